```python
import jax, jax.numpy as jnp
from jax import lax
import numpy as np

D_MODEL = 2048
BATCH = 8
SEQ = 4096
DEPTH = 2
DEC_BATCH = 4
DEC_SEQ = 2048
PAST_LEN = 128

N_MIXERS = 2
N_DN_LAYERS = (DEPTH + N_MIXERS - 1) // N_MIXERS
N_FN_LAYERS = DEPTH // N_MIXERS
D_FF = 5632
DN_HEAD_K = 128
DN_HEAD_V = 128
DN_K_HEADS = D_MODEL // 128
DN_V_HEADS = 2 * DN_K_HEADS
DN_KEY_DIM = DN_K_HEADS * DN_HEAD_K
DN_VALUE_DIM = DN_V_HEADS * DN_HEAD_V
DN_QKV_DIM = 2 * DN_KEY_DIM + DN_VALUE_DIM
DN_GATE_COLS = 4 * DN_V_HEADS
DN_PROJ_DIM = DN_QKV_DIM + DN_VALUE_DIM + DN_GATE_COLS
CONV_WIDTH = 5
CONV_PAD = (CONV_WIDTH - 1) // 2
CHUNK = 64
FN_GROUPS = 8
FN_GROUP_DIM = D_MODEL // FN_GROUPS
EPS = 1e-6

kernel_name = "hybrid_deltanet_fnet_macaron_encoder"


def rms_norm(x, gain):
    xf = x.astype(jnp.float32)
    y = xf * lax.rsqrt(jnp.mean(xf * xf, axis=-1, keepdims=True) + EPS) * gain.astype(jnp.float32)
    return y.astype(x.dtype)


def swiglu(h, w_gate, w_up, w_down):
    return (jax.nn.silu(h @ w_gate) * (h @ w_up)) @ w_down


def l2_normalize(x):
    return x * lax.rsqrt(jnp.sum(x * x, axis=-1, keepdims=True) + EPS)


def centred_depthwise_conv(x, w):
    c = x.shape[-1]
    return lax.conv_general_dilated(
        x, w[:, None, :].astype(x.dtype), window_strides=(1,),
        padding=[(CONV_PAD, CONV_PAD)], dimension_numbers=("NWC", "WIO", "NWC"),
        feature_group_count=c)


def _to_chunks(t, n_chunks):
    b, _, h = t.shape[:3]
    t = t.reshape((b, n_chunks, CHUNK, h) + t.shape[3:])
    return jnp.moveaxis(jnp.moveaxis(t, 1, 0), 2, 3)


def chunk_gated_delta_rule(q, k, v, g, beta):
    b, t, h, dk = q.shape
    dv = v.shape[-1]
    n = t // CHUNK
    qc, kc, vc = _to_chunks(q, n), _to_chunks(k, n), _to_chunks(v, n)
    gc = jnp.cumsum(_to_chunks(g, n), axis=-1)
    bc = _to_chunks(beta, n)
    tril = jnp.tril(jnp.ones((CHUNK, CHUNK), dtype=bool))
    strict = jnp.tril(jnp.ones((CHUNK, CHUNK), dtype=bool), k=-1)
    diff = gc[..., :, None] - gc[..., None, :]
    decay = jnp.where(tril, jnp.exp(jnp.where(tril, diff, 0.0)), 0.0)
    kb = kc * bc[..., None]
    lower = jnp.where(strict, jnp.einsum("nbhid,nbhjd->nbhij", kb, kc) * decay, 0.0)
    eye = jnp.eye(CHUNK, dtype=q.dtype)
    tmat = lax.linalg.triangular_solve(eye + lower, jnp.broadcast_to(eye, lower.shape),
                                       left_side=True, lower=True, unit_diagonal=True)
    u = jnp.einsum("nbhij,nbhjd->nbhid", tmat, vc * bc[..., None])
    w = jnp.einsum("nbhij,nbhjd->nbhid", tmat, kb * jnp.exp(gc)[..., None])
    qk = jnp.where(tril, jnp.einsum("nbhid,nbhjd->nbhij", qc, kc) * decay, 0.0)

    def step(state, xs):
        q_i, k_i, u_i, w_i, qk_i, g_i = xs
        v_new = u_i - jnp.einsum("bhck,bhkv->bhcv", w_i, state)
        o_i = (jnp.einsum("bhck,bhkv->bhcv", q_i * jnp.exp(g_i)[..., None], state)
               + jnp.einsum("bhij,bhjv->bhiv", qk_i, v_new))
        g_last = g_i[..., -1:]
        state = (state * jnp.exp(g_last)[..., None]
                 + jnp.einsum("bhck,bhcv->bhkv", k_i * jnp.exp(g_last - g_i)[..., None], v_new))
        return state, o_i

    s0 = jnp.zeros((b, h, dk, dv), q.dtype)
    _, o = lax.scan(step, s0, (qc, kc, u, w, qk, gc))
    return jnp.transpose(o, (1, 0, 3, 2, 4)).reshape(b, t, h, dv)


def gated_deltanet(h, w_in, conv_w, a_log, dt_bias, out_norm, w_out):
    b, s, _ = h.shape
    proj = h @ w_in
    qkv = proj[..., :DN_QKV_DIM]
    z = proj[..., DN_QKV_DIM:DN_QKV_DIM + DN_VALUE_DIM]
    gates = proj[..., DN_QKV_DIM + DN_VALUE_DIM:].astype(jnp.float32).reshape(b, s, 2, 2, DN_V_HEADS)
    qkv = jax.nn.silu(centred_depthwise_conv(qkv, conv_w)).astype(jnp.float32)
    q = qkv[..., :DN_KEY_DIM].reshape(b, s, DN_K_HEADS, DN_HEAD_K)
    k = qkv[..., DN_KEY_DIM:2 * DN_KEY_DIM].reshape(b, s, DN_K_HEADS, DN_HEAD_K)
    v = qkv[..., 2 * DN_KEY_DIM:].reshape(b, s, DN_V_HEADS, DN_HEAD_V)
    rep = DN_V_HEADS // DN_K_HEADS
    q = jnp.repeat(l2_normalize(q) * (DN_HEAD_K ** -0.5), rep, axis=2)
    k = jnp.repeat(l2_normalize(k), rep, axis=2)
    g = -jnp.exp(a_log.astype(jnp.float32)) * jax.nn.softplus(gates[..., 0, :] + dt_bias.astype(jnp.float32))
    beta = jax.nn.sigmoid(gates[..., 1, :])
    o_fwd = chunk_gated_delta_rule(q, k, v, g[:, :, 0], beta[:, :, 0])
    flip = lambda t: jnp.flip(t, axis=1)
    o_bwd = flip(chunk_gated_delta_rule(flip(q), flip(k), flip(v), flip(g[:, :, 1]), flip(beta[:, :, 1])))
    o = o_fwd + o_bwd
    zf = z.astype(jnp.float32).reshape(b, s, DN_V_HEADS, DN_HEAD_V)
    o = (o * lax.rsqrt(jnp.mean(o * o, axis=-1, keepdims=True) + EPS)
         * out_norm.astype(jnp.float32) * jax.nn.silu(zf))
    return o.reshape(b, s, DN_VALUE_DIM).astype(h.dtype) @ w_out


def fourier_mixer(h, w_out):
    b, s, d = h.shape
    hg = h.astype(jnp.float32).reshape(b, s, FN_GROUPS, FN_GROUP_DIM)
    mixed = jnp.fft.fft2(hg, axes=(1, 3), norm="ortho").real
    return mixed.reshape(b, s, d).astype(h.dtype) @ w_out


def setup_inputs(seed: int = 0) -> dict:
    key = jax.random.key(seed)
    ks = iter(jax.random.split(key, 32))
    nrm = lambda shape, fan_in: jax.random.normal(next(ks), shape, jnp.float32) * (fan_in ** -0.5)
    gain = lambda shape: 1.0 + 0.02 * jax.random.normal(next(ks), shape, jnp.float32)
    x_prompt = jax.random.normal(next(ks), (BATCH, SEQ, D_MODEL), jnp.float32)
    x_sample = jax.random.normal(next(ks), (DEC_BATCH, DEC_SEQ, D_MODEL), jnp.float32)
    ffn1_norm = gain((DEPTH, D_MODEL))
    ffn1_w_gate = nrm((DEPTH, D_MODEL, D_FF), D_MODEL)
    ffn1_w_up = nrm((DEPTH, D_MODEL, D_FF), D_MODEL)
    ffn1_w_down = nrm((DEPTH, D_FF, D_MODEL), D_FF)
    mix_norm = gain((DEPTH, D_MODEL))
    dn_w_in = nrm((N_DN_LAYERS, D_MODEL, DN_PROJ_DIM), D_MODEL)
    dn_conv_w = nrm((N_DN_LAYERS, CONV_WIDTH, DN_QKV_DIM), CONV_WIDTH)
    dn_a_log = jnp.log(jax.random.uniform(next(ks), (N_DN_LAYERS, 2, DN_V_HEADS), jnp.float32, 1.0, 16.0))
    dt = jnp.exp(jax.random.uniform(next(ks), (N_DN_LAYERS, 2, DN_V_HEADS), jnp.float32,
                                    float(np.log(1e-3)), float(np.log(1e-1))))
    dn_dt_bias = dt + jnp.log(-jnp.expm1(-dt))
    dn_out_norm = gain((N_DN_LAYERS, DN_HEAD_V))
    dn_w_out = nrm((N_DN_LAYERS, DN_VALUE_DIM, D_MODEL), DN_VALUE_DIM)
    fn_w_out = nrm((N_FN_LAYERS, D_MODEL, D_MODEL), D_MODEL)
    ffn2_norm = gain((DEPTH, D_MODEL))
    ffn2_w_gate = nrm((DEPTH, D_MODEL, D_FF), D_MODEL)
    ffn2_w_up = nrm((DEPTH, D_MODEL, D_FF), D_MODEL)
    ffn2_w_down = nrm((DEPTH, D_FF, D_MODEL), D_FF)
    final_norm = gain((D_MODEL,))
    return {"x_prompt": x_prompt, "x_sample": x_sample,
            "ffn1_norm": ffn1_norm, "ffn1_w_gate": ffn1_w_gate, "ffn1_w_up": ffn1_w_up, "ffn1_w_down": ffn1_w_down,
            "mix_norm": mix_norm, "dn_w_in": dn_w_in, "dn_conv_w": dn_conv_w, "dn_a_log": dn_a_log,
            "dn_dt_bias": dn_dt_bias, "dn_out_norm": dn_out_norm, "dn_w_out": dn_w_out, "fn_w_out": fn_w_out,
            "ffn2_norm": ffn2_norm, "ffn2_w_gate": ffn2_w_gate, "ffn2_w_up": ffn2_w_up, "ffn2_w_down": ffn2_w_down,
            "final_norm": final_norm}


def reference(x_prompt, x_sample, ffn1_norm, ffn1_w_gate, ffn1_w_up, ffn1_w_down, mix_norm,
              dn_w_in, dn_conv_w, dn_a_log, dn_dt_bias, dn_out_norm, dn_w_out, fn_w_out,
              ffn2_norm, ffn2_w_gate, ffn2_w_up, ffn2_w_down, final_norm):
    def trunk(x):
        for i in range(DEPTH):
            x = x + 0.5 * swiglu(rms_norm(x, ffn1_norm[i]), ffn1_w_gate[i], ffn1_w_up[i], ffn1_w_down[i])
            h = rms_norm(x, mix_norm[i])
            j = i // N_MIXERS
            if i % N_MIXERS == 0:
                x = x + gated_deltanet(h, dn_w_in[j], dn_conv_w[j], dn_a_log[j], dn_dt_bias[j],
                                       dn_out_norm[j], dn_w_out[j])
            else:
                x = x + fourier_mixer(h, fn_w_out[j])
            x = x + 0.5 * swiglu(rms_norm(x, ffn2_norm[i]), ffn2_w_gate[i], ffn2_w_up[i], ffn2_w_down[i])
        return rms_norm(x, final_norm)

    y_prompt = trunk(x_prompt)
    y_sample = trunk(x_sample)
    return (y_prompt, y_sample)
```

```python
import functools
import math

import jax
import jax.numpy as jnp
from jax import lax
from jax.experimental import pallas as pl
from jax.experimental.pallas import tpu as pltpu

EPS = 1e-6
CHUNK = 64
HEAD_DIM = 128
FN_GROUP_DIM = 256
CONV_WIDTH = 5
CONV_PAD = (CONV_WIDTH - 1) // 2
HALO = 8
VMEM_LIMIT_BYTES = 56 * 1024 * 1024
BF16 = jnp.bfloat16
F32 = jnp.float32
HIGHEST = lax.Precision.HIGHEST


def _params(*semantics):
    return pltpu.CompilerParams(dimension_semantics=semantics, vmem_limit_bytes=VMEM_LIMIT_BYTES)


def _rms_scale(x):
    return lax.rsqrt(jnp.mean(x * x, axis=-1, keepdims=True) + EPS)


def _dot(a, b):
    return jnp.dot(a, b, preferred_element_type=F32)


def _dot_nt(a, b):
    return lax.dot_general(a, b, (((1,), (1,)), ((), ())), preferred_element_type=F32)


def _dot_tn(a, b):
    return lax.dot_general(a, b, (((0,), (0,)), ((), ())), preferred_element_type=F32)


def _ffn_body(x_ref, g_ref, wg_ref, wu_ref, wd_ref, fg_ref, o_ref, h_ref, *, n_f, final_norm):
    f = pl.program_id(1)

    @pl.when(f == 0)
    def _():
        x = x_ref[...]
        h_ref[...] = (x * _rms_scale(x) * g_ref[...]).astype(BF16)
        o_ref[...] = x

    h = h_ref[...]
    a = _dot(h, wg_ref[...])
    b = _dot(h, wu_ref[...])
    act = (a * jax.nn.sigmoid(a) * (b * 0.5)).astype(BF16)
    o_ref[...] += _dot(act, wd_ref[...])

    if final_norm:
        @pl.when(f == n_f - 1)
        def _():
            y = o_ref[...]
            o_ref[...] = y * _rms_scale(y) * fg_ref[...]


def _ffn(x, gain, wg, wu, wd, final_gain, *, final_norm, tm=512, tf=512):
    m, d = x.shape
    f_dim = wg.shape[1]
    tm = min(tm, m)
    assert m % tm == 0 and f_dim % tf == 0
    n_f = f_dim // tf
    return pl.pallas_call(
        functools.partial(_ffn_body, n_f=n_f, final_norm=final_norm),
        grid=(m // tm, n_f),
        in_specs=[
            pl.BlockSpec((tm, d), lambda i, f: (i, 0)),
            pl.BlockSpec((1, d), lambda i, f: (0, 0)),
            pl.BlockSpec((d, tf), lambda i, f: (0, f)),
            pl.BlockSpec((d, tf), lambda i, f: (0, f)),
            pl.BlockSpec((tf, d), lambda i, f: (f, 0)),
            pl.BlockSpec((1, d), lambda i, f: (0, 0)),
        ],
        out_specs=pl.BlockSpec((tm, d), lambda i, f: (i, 0)),
        out_shape=jax.ShapeDtypeStruct((m, d), F32),
        scratch_shapes=[pltpu.VMEM((tm, d), BF16)],
        compiler_params=_params("parallel", "arbitrary"),
        name="ffn",
    )(x, gain.reshape(1, d), wg, wu, wd, final_gain.reshape(1, d))


def _norm_mm_body(x_ref, g_ref, w_ref, o_ref, h_ref):
    @pl.when(pl.program_id(1) == 0)
    def _():
        x = x_ref[...]
        h_ref[...] = (x * _rms_scale(x) * g_ref[...]).astype(BF16)

    o_ref[...] = _dot(h_ref[...], w_ref[...]).astype(o_ref.dtype)


def _norm_mm(x, gain, w, *, out_dtype, tm=512, tn=512):
    m, d = x.shape
    n = w.shape[1]
    tm, tn = min(tm, m), min(tn, n)
    assert m % tm == 0 and n % tn == 0
    return pl.pallas_call(
        _norm_mm_body,
        grid=(m // tm, n // tn),
        in_specs=[
            pl.BlockSpec((tm, d), lambda i, j: (i, 0)),
            pl.BlockSpec((1, d), lambda i, j: (0, 0)),
            pl.BlockSpec((d, tn), lambda i, j: (0, j)),
        ],
        out_specs=pl.BlockSpec((tm, tn), lambda i, j: (i, j)),
        out_shape=jax.ShapeDtypeStruct((m, n), out_dtype),
        scratch_shapes=[pltpu.VMEM((tm, d), BF16)],
        compiler_params=_params("parallel", "arbitrary"),
        name="norm_mm",
    )(x, gain.reshape(1, d), w)


def _mm_res_body(a_ref, w_ref, r_ref, o_ref, acc_ref, *, n_k):
    k = pl.program_id(2)
    p = _dot(a_ref[...], w_ref[...])

    @pl.when(k == 0)
    def _():
        acc_ref[...] = p

    @pl.when(k > 0)
    def _():
        acc_ref[...] += p

    @pl.when(k == n_k - 1)
    def _():
        o_ref[...] = r_ref[...] + acc_ref[...]


def _mm_res(a, w, res, *, tm=512, tn=1024, tk=2048):
    m, kd = a.shape
    n = w.shape[1]
    tm, tn, tk = min(tm, m), min(tn, n), min(tk, kd)
    assert m % tm == 0 and n % tn == 0 and kd % tk == 0
    n_k = kd // tk
    return pl.pallas_call(
        functools.partial(_mm_res_body, n_k=n_k),
        grid=(m // tm, n // tn, n_k),
        in_specs=[
            pl.BlockSpec((tm, tk), lambda i, j, k: (i, k)),
            pl.BlockSpec((tk, tn), lambda i, j, k: (k, j)),
            pl.BlockSpec((tm, tn), lambda i, j, k: (i, j)),
        ],
        out_specs=pl.BlockSpec((tm, tn), lambda i, j, k: (i, j)),
        out_shape=jax.ShapeDtypeStruct((m, n), F32),
        scratch_shapes=[pltpu.VMEM((tm, tn), F32)],
        compiler_params=_params("parallel", "parallel", "arbitrary"),
        name="mm_res",
    )(a, w, res)


def _gates_body(x_ref, alog_ref, dtb_ref, ga_ref, gt_ref, *, n_chunks):
    lane = lax.broadcasted_iota(jnp.int32, (CHUNK, 128), 1)
    is_decay = (lane % 64) < 32
    is_bwd = lane >= 64
    row = lax.broadcasted_iota(jnp.int32, (CHUNK, CHUNK), 0)
    col = lax.broadcasted_iota(jnp.int32, (CHUNK, CHUNK), 1)
    lower = (row >= col).astype(F32)
    upper = (row <= col).astype(F32)
    ones = jnp.ones((CHUNK, CHUNK), F32)
    neg_a = -jnp.exp(alog_ref[...])
    dtb = dtb_ref[...]
    for c in range(n_chunks):
        x = x_ref[0, c * CHUNK:(c + 1) * CHUNK, :]
        t = x + dtb
        softplus = jnp.maximum(t, 0.0) + jnp.log(1.0 + jnp.exp(-jnp.abs(t)))
        g = jnp.where(is_decay, neg_a * softplus, 0.0)
        beta = jax.nn.sigmoid(x)
        cum_f = jnp.dot(lower, g, precision=HIGHEST, preferred_element_type=F32)
        cum_b = jnp.dot(upper, g, precision=HIGHEST, preferred_element_type=F32)
        tot = jnp.dot(ones, g, precision=HIGHEST, preferred_element_type=F32)
        cum = jnp.where(is_bwd, cum_b, cum_f)
        ga_ref[0, c * CHUNK:(c + 1) * CHUNK, :] = jnp.where(is_decay, cum, beta)
        gt_ref[0, c * CHUNK:(c + 1) * CHUNK, :] = tot


def _gates(gates, a_log, dt_bias, *, tb=512):
    b, t, w = gates.shape
    tb = min(tb, t)
    assert t % tb == 0 and tb % CHUNK == 0 and w == 128
    zeros = jnp.zeros((32,), F32)
    alog_l = jnp.concatenate([a_log[0], zeros, a_log[1], zeros]).reshape(1, 128).astype(F32)
    dtb_l = jnp.concatenate([dt_bias[0], zeros, dt_bias[1], zeros]).reshape(1, 128).astype(F32)
    spec = pl.BlockSpec((1, tb, 128), lambda i, j: (i, j, 0))
    vec = pl.BlockSpec((1, 128), lambda i, j: (0, 0))
    return pl.pallas_call(
        functools.partial(_gates_body, n_chunks=tb // CHUNK),
        grid=(b, t // tb),
        in_specs=[spec, vec, vec],
        out_specs=[spec, spec],
        out_shape=[jax.ShapeDtypeStruct((b, t, 128), F32)] * 2,
        compiler_params=_params("parallel", "parallel"),
        name="gates",
    )(gates, alog_l, dtb_l)


def _conv_body(x_ref, w_ref, o_ref, pad_ref, *, t, rows, l2_scale):
    lanes = x_ref.shape[-1]
    zero = jnp.zeros((HALO, lanes), F32)
    pad_ref[0:HALO, :] = zero
    pad_ref[HALO + t:HALO + t + HALO, :] = zero
    pad_ref[HALO:HALO + t, :] = x_ref[0]
    w = w_ref[...]

    def tile(i, carry):
        r0 = pl.multiple_of(i * rows, rows)
        win = pad_ref[pl.ds(r0, rows + 2 * HALO), :]
        acc = None
        for j in range(CONV_WIDTH):
            s = HALO - CONV_PAD + j
            term = win[s:s + rows, :] * w[j:j + 1, :]
            acc = term if acc is None else acc + term
        y = acc * jax.nn.sigmoid(acc)
        if l2_scale is not None:
            parts = []
            for h in range(lanes // HEAD_DIM):
                yh = y[:, h * HEAD_DIM:(h + 1) * HEAD_DIM]
                inv = lax.rsqrt(jnp.sum(yh * yh, axis=-1, keepdims=True) + EPS)
                parts.append(yh * (inv * l2_scale))
            y = jnp.concatenate(parts, axis=-1)
        o_ref[0, pl.ds(r0, rows), :] = y.astype(o_ref.dtype)
        return carry

    lax.fori_loop(0, t // rows, tile, 0)


def _conv(proj, conv_w, *, col0, width, l2_scale, lanes=256, rows=256):
    b, t, _ = proj.shape
    rows = min(rows, t)
    assert width % lanes == 0 and col0 % lanes == 0 and t % rows == 0
    off = col0 // lanes
    w8 = jnp.zeros((8, width), F32).at[:CONV_WIDTH].set(conv_w.astype(F32))
    return pl.pallas_call(
        functools.partial(_conv_body, t=t, rows=rows, l2_scale=l2_scale),
        grid=(b, width // lanes),
        in_specs=[
            pl.BlockSpec((1, t, lanes), lambda i, j: (i, 0, j + off)),
            pl.BlockSpec((8, lanes), lambda i, j: (0, j)),
        ],
        out_specs=pl.BlockSpec((1, t, lanes), lambda i, j: (i, 0, j)),
        out_shape=jax.ShapeDtypeStruct((b, t, width), F32),
        scratch_shapes=[pltpu.VMEM((t + 2 * HALO, lanes), F32)],
        compiler_params=_params("parallel", "parallel"),
        name="conv",
    )(proj, w8)


def _unit_tri_inverse(lmat):
    row = lax.broadcasted_iota(jnp.int32, (CHUNK, CHUNK), 0)
    col = lax.broadcasted_iota(jnp.int32, (CHUNK, CHUNK), 1)
    eye = (row == col).astype(F32)
    x = -lmat
    p = eye + x
    for _ in range(int(math.log2(CHUNK)) - 1):
        x = jnp.dot(x, x, precision=HIGHEST, preferred_element_type=F32)
        p = p + jnp.dot(p, x, precision=HIGHEST, preferred_element_type=F32)
    return p


def _delta_body(q_ref, k_ref, v_ref, z_ref, ga_ref, gt_ref, gain_ref, y_ref, o_ref, s_ref,
                *, n_chunks, rows):
    kh = pl.program_id(1)
    t = n_chunks * CHUNK
    o_ref[...] = jnp.zeros_like(o_ref)
    s_ref[...] = jnp.zeros_like(s_ref)

    row = lax.broadcasted_iota(jnp.int32, (CHUNK, CHUNK), 0)
    col = lax.broadcasted_iota(jnp.int32, (CHUNK, CHUNK), 1)
    sel_r = lax.broadcasted_iota(jnp.int32, (128, HEAD_DIM), 0)
    sel_c = lax.broadcasted_iota(jnp.int32, (CHUNK, 128), 1)

    def chunk_step(c, carry):
        for d in range(2):
            cc = c if d == 0 else n_chunks - 1 - c
            r0 = pl.multiple_of(cc * CHUNK, CHUNK)
            incl = (row >= col) if d == 0 else (row <= col)
            strict = (row > col) if d == 0 else (row < col)
            q = q_ref[0, pl.ds(r0, CHUNK), :]
            k = k_ref[0, pl.ds(r0, CHUNK), :]
            ga = ga_ref[0, pl.ds(r0, CHUNK), :]
            gt = gt_ref[0, pl.ds(r0, CHUNK), :]
            kb16 = k.astype(BF16)
            kk = _dot_nt(kb16, kb16)
            qk = _dot_nt(q.astype(BF16), kb16)
            for j in range(2):
                hv = 2 * kh + j
                lane_g = d * 64 + hv
                lane_b = lane_g + 32
                e_g = (sel_r == lane_g).astype(F32)
                e_b = (sel_r == lane_b).astype(F32)
                a_g = (sel_c == lane_g).astype(F32)
                gcol = jnp.dot(ga, e_g, precision=HIGHEST, preferred_element_type=F32)
                bcol = jnp.dot(ga, e_b, precision=HIGHEST, preferred_element_type=F32)
                tcol = jnp.dot(gt, e_g, precision=HIGHEST, preferred_element_type=F32)
                grow = lax.dot_general(a_g, ga, (((1,), (1,)), ((), ())), precision=HIGHEST,
                                       preferred_element_type=F32)
                diff = gcol[:, :CHUNK] - grow
                decay = jnp.where(incl, jnp.exp(jnp.where(incl, diff, 0.0)), 0.0)
                lmat = jnp.where(strict, bcol[:, :CHUNK] * kk * decay, 0.0)
                tmat = _unit_tri_inverse(lmat)
                v = v_ref[0, pl.ds(r0, CHUNK), j * HEAD_DIM:(j + 1) * HEAD_DIM]
                eg = jnp.exp(gcol)
                kbeta = k * bcol
                rhs = jnp.concatenate([v * bcol, kbeta * eg], axis=1).astype(BF16)
                uw = _dot(tmat.astype(BF16), rhs)
                u = uw[:, :HEAD_DIM]
                w = uw[:, HEAD_DIM:]
                qkm = jnp.where(incl, qk * decay, 0.0)
                idx = d * 2 + j
                state = s_ref[idx]
                s16 = state.astype(BF16)
                lhs = jnp.concatenate([w, q * eg], axis=0).astype(BF16)
                ws = _dot(lhs, s16)
                v_new = u - ws[:CHUNK]
                vn16 = v_new.astype(BF16)
                o = ws[CHUNK:] + _dot(qkm.astype(BF16), vn16)
                kdec = (k * jnp.exp(tcol - gcol)).astype(BF16)
                s_ref[idx] = state * jnp.exp(tcol[0:1, :]) + _dot_tn(kdec, vn16)
                o_ref[pl.ds(r0, CHUNK), j * HEAD_DIM:(j + 1) * HEAD_DIM] += o
        return carry

    lax.fori_loop(0, n_chunks, chunk_step, 0)

    gain = gain_ref[...]

    def norm_tile(i, carry):
        r0 = pl.multiple_of(i * rows, rows)
        o = o_ref[pl.ds(r0, rows), :]
        z = z_ref[0, pl.ds(r0, rows), :]
        parts = []
        for j in range(2):
            oh = o[:, j * HEAD_DIM:(j + 1) * HEAD_DIM]
            parts.append(oh * _rms_scale(oh) * gain)
        y = jnp.concatenate(parts, axis=-1) * (z * jax.nn.sigmoid(z))
        y_ref[0, pl.ds(r0, rows), :] = y.astype(y_ref.dtype)
        return carry

    lax.fori_loop(0, t // rows, norm_tile, 0)


def _delta(q, k, v, proj, z_col0, ga, gt, out_gain, *, rows=256):
    b, t, kdim = q.shape
    n_kh = kdim // HEAD_DIM
    vdim = v.shape[-1]
    assert vdim == 2 * kdim and t % CHUNK == 0 and z_col0 % (2 * HEAD_DIM) == 0
    rows = min(rows, t)
    z_off = z_col0 // (2 * HEAD_DIM)
    return pl.pallas_call(
        functools.partial(_delta_body, n_chunks=t // CHUNK, rows=rows),
        grid=(b, n_kh),
        in_specs=[
            pl.BlockSpec((1, t, HEAD_DIM), lambda i, h: (i, 0, h)),
            pl.BlockSpec((1, t, HEAD_DIM), lambda i, h: (i, 0, h)),
            pl.BlockSpec((1, t, 2 * HEAD_DIM), lambda i, h: (i, 0, h)),
            pl.BlockSpec((1, t, 2 * HEAD_DIM), lambda i, h: (i, 0, h + z_off)),
            pl.BlockSpec((1, t, 128), lambda i, h: (i, 0, 0)),
            pl.BlockSpec((1, t, 128), lambda i, h: (i, 0, 0)),
            pl.BlockSpec((1, HEAD_DIM), lambda i, h: (0, 0)),
        ],
        out_specs=pl.BlockSpec((1, t, 2 * HEAD_DIM), lambda i, h: (i, 0, h)),
        out_shape=jax.ShapeDtypeStruct((b, t, vdim), BF16),
        scratch_shapes=[pltpu.VMEM((t, 2 * HEAD_DIM), F32), pltpu.VMEM((4, HEAD_DIM, HEAD_DIM), F32)],
        compiler_params=_params("parallel", "arbitrary"),
        name="delta",
    )(q, k, v, proj, ga, gt, out_gain.reshape(1, HEAD_DIM).astype(F32))


def _gated_deltanet(x, norm_gain, w_in, conv_w, a_log, dt_bias, out_norm, w_out):
    b, s, d = x.shape
    n_gate = 128
    qkvz_dim = w_in.shape[1] - n_gate
    value_dim = w_out.shape[0]
    qkv_dim = qkvz_dim - value_dim
    key_dim = (qkv_dim - value_dim) // 2
    x2 = x.reshape(b * s, d)
    w16 = w_in.astype(BF16)
    proj = _norm_mm(x2, norm_gain, w16[:, :qkvz_dim], out_dtype=F32).reshape(b, s, qkvz_dim)
    gates = _norm_mm(x2, norm_gain, w16[:, qkvz_dim:], out_dtype=F32).reshape(b, s, n_gate)
    ga, gt = _gates(gates, a_log, dt_bias)
    q = _conv(proj, conv_w[:, :key_dim], col0=0, width=key_dim, l2_scale=HEAD_DIM ** -0.5)
    k = _conv(proj, conv_w[:, key_dim:2 * key_dim], col0=key_dim, width=key_dim, l2_scale=1.0)
    v = _conv(proj, conv_w[:, 2 * key_dim:], col0=2 * key_dim, width=value_dim, l2_scale=None)
    y = _delta(q, k, v, proj, qkv_dim, ga, gt, out_norm)
    return _mm_res(y.reshape(b * s, value_dim), w_out.astype(BF16), x2).reshape(b, s, d)


def _chan_dft_body(x_ref, g_ref, cs_ref, o_ref):
    x = x_ref[0]
    h = (x * _rms_scale(x) * g_ref[...]).astype(BF16)
    cs = cs_ref[...]
    for g in range(x.shape[-1] // FN_GROUP_DIM):
        lo, hi = g * FN_GROUP_DIM, (g + 1) * FN_GROUP_DIM
        pq = _dot(h[:, lo:hi], cs)
        o_ref[0, 0, :, lo:hi] = pq[:, :FN_GROUP_DIM].astype(o_ref.dtype)
        o_ref[0, 1, :, lo:hi] = pq[:, FN_GROUP_DIM:].astype(o_ref.dtype)


def _chan_dft(x, gain, cs, *, tm=512):
    b, s, d = x.shape
    tm = min(tm, s)
    assert s % tm == 0 and d % FN_GROUP_DIM == 0
    return pl.pallas_call(
        _chan_dft_body,
        grid=(b, s // tm),
        in_specs=[
            pl.BlockSpec((1, tm, d), lambda i, j: (i, j, 0)),
            pl.BlockSpec((1, d), lambda i, j: (0, 0)),
            pl.BlockSpec((FN_GROUP_DIM, 2 * FN_GROUP_DIM), lambda i, j: (0, 0)),
        ],
        out_specs=pl.BlockSpec((1, 2, tm, d), lambda i, j: (i, 0, j, 0)),
        out_shape=jax.ShapeDtypeStruct((b, 2, s, d), BF16),
        compiler_params=_params("parallel", "parallel"),
        name="chan_dft",
    )(x, gain.reshape(1, d), cs)


def _seq_dft_body(a_ref, b_ref, o_ref, acc_ref, *, n_k, scale):
    k = pl.program_id(3)
    p = _dot(a_ref[...], b_ref[0])

    @pl.when(k == 0)
    def _():
        acc_ref[...] = p

    @pl.when(k > 0)
    def _():
        acc_ref[...] += p

    @pl.when(k == n_k - 1)
    def _():
        o_ref[0] = (acc_ref[...] * scale).astype(o_ref.dtype)


def _seq_dft(dft, pq, *, scale, tm=2048, tn=1024, tk=1024):
    s, k2 = dft.shape
    b, _, d = pq.shape
    tm, tn, tk = min(tm, s), min(tn, d), min(tk, k2)
    assert s % tm == 0 and d % tn == 0 and k2 % tk == 0
    n_k = k2 // tk
    return pl.pallas_call(
        functools.partial(_seq_dft_body, n_k=n_k, scale=scale),
        grid=(b, s // tm, d // tn, n_k),
        in_specs=[
            pl.BlockSpec((tm, tk), lambda i, m, n, k: (m, k)),
            pl.BlockSpec((1, tk, tn), lambda i, m, n, k: (i, k, n)),
        ],
        out_specs=pl.BlockSpec((1, tm, tn), lambda i, m, n, k: (i, m, n)),
        out_shape=jax.ShapeDtypeStruct((b, s, d), BF16),
        scratch_shapes=[pltpu.VMEM((tm, tn), F32)],
        compiler_params=_params("parallel", "parallel", "parallel", "arbitrary"),
        name="seq_dft",
    )(dft, pq)


def _dft_tables(n):
    idx = jnp.arange(n, dtype=jnp.int32)
    prod = (idx[:, None] * idx[None, :]) % n
    theta = prod.astype(F32) * (2.0 * math.pi / n)
    return jnp.cos(theta), jnp.sin(theta)


def _fourier_mixer(x, norm_gain, w_out):
    b, s, d = x.shape
    cc, sc = _dft_tables(FN_GROUP_DIM)
    cs_c = jnp.concatenate([cc, sc], axis=1).astype(BF16)
    cs_s, sn_s = _dft_tables(s)
    dft = jnp.concatenate([cs_s, -sn_s], axis=1).astype(BF16)
    pq = _chan_dft(x, norm_gain, cs_c).reshape(b, 2 * s, d)
    mixed = _seq_dft(dft, pq, scale=float((s * FN_GROUP_DIM) ** -0.5))
    return _mm_res(mixed.reshape(b * s, d), w_out.astype(BF16), x.reshape(b * s, d)).reshape(b, s, d)


def kernel(x_prompt, x_sample, ffn1_norm, ffn1_w_gate, ffn1_w_up, ffn1_w_down, mix_norm, dn_w_in, dn_conv_w, dn_a_log, dn_dt_bias, dn_out_norm, dn_w_out, fn_w_out, ffn2_norm, ffn2_w_gate, ffn2_w_up, ffn2_w_down, final_norm):
    depth = ffn1_norm.shape[0]
    n_mixers = 2
    f1 = [(ffn1_w_gate[i].astype(BF16), ffn1_w_up[i].astype(BF16), ffn1_w_down[i].astype(BF16)) for i in range(depth)]
    f2 = [(ffn2_w_gate[i].astype(BF16), ffn2_w_up[i].astype(BF16), ffn2_w_down[i].astype(BF16)) for i in range(depth)]

    def ffn(x, gain, ws, final):
        b, s, d = x.shape
        return _ffn(x.reshape(b * s, d), gain, *ws, final_norm, final_norm=final).reshape(b, s, d)

    def trunk(x):
        for i in range(depth):
            x = ffn(x, ffn1_norm[i], f1[i], False)
            j = i // n_mixers
            if i % n_mixers == 0:
                x = _gated_deltanet(x, mix_norm[i], dn_w_in[j], dn_conv_w[j], dn_a_log[j], dn_dt_bias[j],
                                    dn_out_norm[j], dn_w_out[j])
            else:
                x = _fourier_mixer(x, mix_norm[i], fn_w_out[j])
            x = ffn(x, ffn2_norm[i], f2[i], i == depth - 1)
        return x

    return (trunk(x_prompt), trunk(x_sample))
```

```python
import functools
import math

import jax
import jax.numpy as jnp
from jax import lax
from jax.experimental import pallas as pl
from jax.experimental.pallas import tpu as pltpu

EPS = 1e-6
CHUNK = 64
HEAD_DIM = 128
FN_GROUP_DIM = 256
CONV_WIDTH = 5
CONV_PAD = (CONV_WIDTH - 1) // 2
HALO = 8
VMEM_LIMIT_BYTES = 56 * 1024 * 1024
BF16 = jnp.bfloat16
F32 = jnp.float32
HIGHEST = lax.Precision.HIGHEST


def _params(*semantics):
    return pltpu.CompilerParams(dimension_semantics=semantics, vmem_limit_bytes=VMEM_LIMIT_BYTES)


def _rms_scale(x):
    return lax.rsqrt(jnp.mean(x * x, axis=-1, keepdims=True) + EPS)


def _dot(a, b):
    return jnp.dot(a, b, preferred_element_type=F32)


def _dot_nt(a, b):
    return lax.dot_general(a, b, (((1,), (1,)), ((), ())), preferred_element_type=F32)


def _dot_tn(a, b):
    return lax.dot_general(a, b, (((0,), (0,)), ((), ())), preferred_element_type=F32)


def _ffn_body(x_ref, g_ref, wg_ref, wu_ref, wd_ref, fg_ref, o_ref, h_ref, *, n_f, final_norm):
    f = pl.program_id(1)

    @pl.when(f == 0)
    def _():
        x = x_ref[...]
        h_ref[...] = (x * _rms_scale(x) * g_ref[...]).astype(BF16)
        o_ref[...] = x

    h = h_ref[...]
    a = _dot(h, wg_ref[...])
    b = _dot(h, wu_ref[...])
    act = (a * jax.nn.sigmoid(a) * (b * 0.5)).astype(BF16)
    o_ref[...] += _dot(act, wd_ref[...])

    if final_norm:
        @pl.when(f == n_f - 1)
        def _():
            y = o_ref[...]
            o_ref[...] = y * _rms_scale(y) * fg_ref[...]


def _ffn(x, gain, wg, wu, wd, final_gain, *, final_norm, tm=512, tf=512):
    m, d = x.shape
    f_dim = wg.shape[1]
    tm = min(tm, m)
    assert m % tm == 0 and f_dim % tf == 0
    n_f = f_dim // tf
    return pl.pallas_call(
        functools.partial(_ffn_body, n_f=n_f, final_norm=final_norm),
        grid=(m // tm, n_f),
        in_specs=[
            pl.BlockSpec((tm, d), lambda i, f: (i, 0)),
            pl.BlockSpec((1, d), lambda i, f: (0, 0)),
            pl.BlockSpec((d, tf), lambda i, f: (0, f)),
            pl.BlockSpec((d, tf), lambda i, f: (0, f)),
            pl.BlockSpec((tf, d), lambda i, f: (f, 0)),
            pl.BlockSpec((1, d), lambda i, f: (0, 0)),
        ],
        out_specs=pl.BlockSpec((tm, d), lambda i, f: (i, 0)),
        out_shape=jax.ShapeDtypeStruct((m, d), F32),
        scratch_shapes=[pltpu.VMEM((tm, d), BF16)],
        compiler_params=_params("parallel", "arbitrary"),
        name="ffn",
    )(x, gain.reshape(1, d), wg, wu, wd, final_gain.reshape(1, d))


def _norm_mm_body(x_ref, g_ref, w_ref, o_ref, h_ref):
    @pl.when(pl.program_id(1) == 0)
    def _():
        x = x_ref[...]
        h_ref[...] = (x * _rms_scale(x) * g_ref[...]).astype(BF16)

    o_ref[...] = _dot(h_ref[...], w_ref[...]).astype(o_ref.dtype)


def _norm_mm(x, gain, w, *, out_dtype, tm=512, tn=512):
    m, d = x.shape
    n = w.shape[1]
    tm, tn = min(tm, m), min(tn, n)
    assert m % tm == 0 and n % tn == 0
    return pl.pallas_call(
        _norm_mm_body,
        grid=(m // tm, n // tn),
        in_specs=[
            pl.BlockSpec((tm, d), lambda i, j: (i, 0)),
            pl.BlockSpec((1, d), lambda i, j: (0, 0)),
            pl.BlockSpec((d, tn), lambda i, j: (0, j)),
        ],
        out_specs=pl.BlockSpec((tm, tn), lambda i, j: (i, j)),
        out_shape=jax.ShapeDtypeStruct((m, n), out_dtype),
        scratch_shapes=[pltpu.VMEM((tm, d), BF16)],
        compiler_params=_params("parallel", "arbitrary"),
        name="norm_mm",
    )(x, gain.reshape(1, d), w)


def _mm_res_body(a_ref, w_ref, r_ref, o_ref, acc_ref, *, n_k):
    k = pl.program_id(2)
    p = _dot(a_ref[...], w_ref[...])

    @pl.when(k == 0)
    def _():
        acc_ref[...] = p

    @pl.when(k > 0)
    def _():
        acc_ref[...] += p

    @pl.when(k == n_k - 1)
    def _():
        o_ref[...] = r_ref[...] + acc_ref[...]


def _mm_res(a, w, res, *, tm=512, tn=1024, tk=2048):
    m, kd = a.shape
    n = w.shape[1]
    tm, tn, tk = min(tm, m), min(tn, n), min(tk, kd)
    assert m % tm == 0 and n % tn == 0 and kd % tk == 0
    n_k = kd // tk
    return pl.pallas_call(
        functools.partial(_mm_res_body, n_k=n_k),
        grid=(m // tm, n // tn, n_k),
        in_specs=[
            pl.BlockSpec((tm, tk), lambda i, j, k: (i, k)),
            pl.BlockSpec((tk, tn), lambda i, j, k: (k, j)),
            pl.BlockSpec((tm, tn), lambda i, j, k: (i, j)),
        ],
        out_specs=pl.BlockSpec((tm, tn), lambda i, j, k: (i, j)),
        out_shape=jax.ShapeDtypeStruct((m, n), F32),
        scratch_shapes=[pltpu.VMEM((tm, tn), F32)],
        compiler_params=_params("parallel", "parallel", "arbitrary"),
        name="mm_res",
    )(a, w, res)


def _gates_body(x_ref, alog_ref, dtb_ref, ga_ref, *, n_chunks):
    lane = lax.broadcasted_iota(jnp.int32, (CHUNK, 128), 1)
    is_decay = (lane % 64) < 32
    is_bwd = lane >= 64
    row = lax.broadcasted_iota(jnp.int32, (CHUNK, CHUNK), 0)
    col = lax.broadcasted_iota(jnp.int32, (CHUNK, CHUNK), 1)
    lower = (row >= col).astype(F32)
    upper = (row <= col).astype(F32)
    neg_a = -jnp.exp(alog_ref[...])
    dtb = dtb_ref[...]
    for c in range(n_chunks):
        x = x_ref[0, c * CHUNK:(c + 1) * CHUNK, :]
        t = x + dtb
        softplus = jnp.maximum(t, 0.0) + jnp.log(1.0 + jnp.exp(-jnp.abs(t)))
        g = jnp.where(is_decay, neg_a * softplus, 0.0)
        beta = jax.nn.sigmoid(x)
        cum_f = jnp.dot(lower, g, precision=HIGHEST, preferred_element_type=F32)
        cum_b = jnp.dot(upper, g, precision=HIGHEST, preferred_element_type=F32)
        cum = jnp.where(is_bwd, cum_b, cum_f)
        ga_ref[0, c * CHUNK:(c + 1) * CHUNK, :] = jnp.where(is_decay, cum, beta)


def _gates(gates, a_log, dt_bias, *, tb=512):
    b, t, w = gates.shape
    tb = min(tb, t)
    assert t % tb == 0 and tb % CHUNK == 0 and w == 128
    zeros = jnp.zeros((32,), F32)
    alog_l = jnp.concatenate([a_log[0], zeros, a_log[1], zeros]).reshape(1, 128).astype(F32)
    dtb_l = jnp.concatenate([dt_bias[0], zeros, dt_bias[1], zeros]).reshape(1, 128).astype(F32)
    spec = pl.BlockSpec((1, tb, 128), lambda i, j: (i, j, 0))
    vec = pl.BlockSpec((1, 128), lambda i, j: (0, 0))
    return pl.pallas_call(
        functools.partial(_gates_body, n_chunks=tb // CHUNK),
        grid=(b, t // tb),
        in_specs=[spec, vec, vec],
        out_specs=spec,
        out_shape=jax.ShapeDtypeStruct((b, t, 128), F32),
        compiler_params=_params("parallel", "parallel"),
        name="gates",
    )(gates, alog_l, dtb_l)


def _conv_body(x_ref, w_ref, o_ref, pad_ref, *, t, rows, l2_scale):
    lanes = x_ref.shape[-1]
    zero = jnp.zeros((HALO, lanes), F32)
    pad_ref[0:HALO, :] = zero
    pad_ref[HALO + t:HALO + t + HALO, :] = zero
    pad_ref[HALO:HALO + t, :] = x_ref[0]
    w = w_ref[...]

    def tile(i, carry):
        r0 = pl.multiple_of(i * rows, rows)
        win = pad_ref[pl.ds(r0, rows + 2 * HALO), :]
        acc = None
        for j in range(CONV_WIDTH):
            s = HALO - CONV_PAD + j
            term = win[s:s + rows, :] * w[j:j + 1, :]
            acc = term if acc is None else acc + term
        y = acc * jax.nn.sigmoid(acc)
        if l2_scale is not None:
            parts = []
            for h in range(lanes // HEAD_DIM):
                yh = y[:, h * HEAD_DIM:(h + 1) * HEAD_DIM]
                inv = lax.rsqrt(jnp.sum(yh * yh, axis=-1, keepdims=True) + EPS)
                parts.append(yh * (inv * l2_scale))
            y = jnp.concatenate(parts, axis=-1)
        o_ref[0, pl.ds(r0, rows), :] = y.astype(o_ref.dtype)
        return carry

    lax.fori_loop(0, t // rows, tile, 0)


def _conv(proj, conv_w, *, col0, width, l2_scale, lanes=256, rows=256):
    b, t, _ = proj.shape
    rows = min(rows, t)
    assert width % lanes == 0 and col0 % lanes == 0 and t % rows == 0
    off = col0 // lanes
    w8 = jnp.zeros((8, width), F32).at[:CONV_WIDTH].set(conv_w.astype(F32))
    return pl.pallas_call(
        functools.partial(_conv_body, t=t, rows=rows, l2_scale=l2_scale),
        grid=(b, width // lanes),
        in_specs=[
            pl.BlockSpec((1, t, lanes), lambda i, j: (i, 0, j + off)),
            pl.BlockSpec((8, lanes), lambda i, j: (0, j)),
        ],
        out_specs=pl.BlockSpec((1, t, lanes), lambda i, j: (i, 0, j)),
        out_shape=jax.ShapeDtypeStruct((b, t, width), F32),
        scratch_shapes=[pltpu.VMEM((t + 2 * HALO, lanes), F32)],
        compiler_params=_params("parallel", "parallel"),
        name="conv",
    )(proj, w8)


N_SUB = 4
SUPER = N_SUB * CHUNK


def _pack(mat):
    blk = lax.broadcasted_iota(jnp.int32, (CHUNK, SUPER), 1) // CHUNK
    out = mat[(N_SUB - 1) * CHUNK:, :]
    for n in range(N_SUB - 2, -1, -1):
        out = jnp.where(blk == n, mat[n * CHUNK:(n + 1) * CHUNK, :], out)
    return out


def _pack_col(colv):
    blk = lax.broadcasted_iota(jnp.int32, (CHUNK, SUPER), 1) // CHUNK
    out = jnp.broadcast_to(colv[(N_SUB - 1) * CHUNK:, :], (CHUNK, SUPER))
    for n in range(N_SUB - 2, -1, -1):
        out = jnp.where(blk == n, jnp.broadcast_to(colv[n * CHUNK:(n + 1) * CHUNK, :], (CHUNK, SUPER)), out)
    return out


def _block_diag(packed):
    r = lax.broadcasted_iota(jnp.int32, (SUPER, SUPER), 0) // CHUNK
    c = lax.broadcasted_iota(jnp.int32, (SUPER, SUPER), 1) // CHUNK
    return jnp.where(r == c, jnp.concatenate([packed] * N_SUB, axis=0), jnp.zeros((), packed.dtype))


def _unit_tri_inverse_packed(lps):
    i = lax.broadcasted_iota(jnp.int32, (CHUNK, SUPER), 0)
    j = lax.broadcasted_iota(jnp.int32, (CHUNK, SUPER), 1) % CHUNK
    eye = jnp.where(i == j, 1.0, 0.0)
    xs = [-lp for lp in lps]
    ps = [eye + x for x in xs]
    n_levels = int(math.log2(CHUNK)) - 1
    xs = [_dot(x.astype(BF16), _block_diag(x.astype(BF16))) for x in xs]
    for lvl in range(n_levels):
        rhss = [_block_diag(x.astype(BF16)) for x in xs]
        if lvl == n_levels - 1:
            ps = [p + _dot(p.astype(BF16), rhs) for p, rhs in zip(ps, rhss)]
        else:
            boths = [_dot(jnp.concatenate([x, p], axis=0).astype(BF16), rhs) for x, p, rhs in zip(xs, ps, rhss)]
            xs = [both[:CHUNK] for both in boths]
            ps = [p + both[CHUNK:] for p, both in zip(ps, boths)]
    return ps


def _delta_body(q_ref, k_ref, v_ref, z_ref, ga_ref, gr_ref, gain_ref, y_ref, o_ref, s_ref,
                *, n_super, rows):
    kh = pl.program_id(1)
    t = n_super * SUPER
    o_ref[...] = jnp.zeros_like(o_ref)
    s_ref[...] = jnp.zeros_like(s_ref)

    pi = lax.broadcasted_iota(jnp.int32, (CHUNK, SUPER), 0)
    pj = lax.broadcasted_iota(jnp.int32, (CHUNK, SUPER), 1) % CHUNK
    gate_lane = lax.broadcasted_iota(jnp.int32, (SUPER, 128), 1)
    row_blk = lax.broadcasted_iota(jnp.int32, (SUPER, 1), 0) // CHUNK

    def super_step(s, carry):
        chains = []
        for d in range(2):
            ss = s if d == 0 else n_super - 1 - s
            r0 = pl.multiple_of(ss * SUPER, SUPER)
            incl = (pi >= pj) if d == 0 else (pi <= pj)
            strict = (pi > pj) if d == 0 else (pi < pj)
            q = q_ref[0, pl.ds(r0, SUPER), :]
            k = k_ref[0, pl.ds(r0, SUPER), :]
            ga = ga_ref[0, pl.ds(r0, SUPER), :]
            grows = gr_ref[0, 0, ss]
            k16 = k.astype(BF16)
            gram = _dot_nt(jnp.concatenate([k16, q.astype(BF16)], axis=0), k16)
            kk_p = _pack(gram[:SUPER])
            qk_p = _pack(gram[SUPER:])
            for j in range(2):
                lane_g = d * 64 + 2 * kh + j
                gcol = jnp.sum(jnp.where(gate_lane == lane_g, ga, 0.0), axis=-1, keepdims=True)
                bcol = jnp.sum(jnp.where(gate_lane == lane_g + 32, ga, 0.0), axis=-1, keepdims=True)
                grow = grows[d * 2 + j:d * 2 + j + 1, :]
                diff = _pack_col(gcol) - grow
                decay = jnp.where(incl, jnp.exp(jnp.where(incl, diff, 0.0)), 0.0)
                eg = jnp.exp(gcol)
                v = v_ref[0, pl.ds(r0, SUPER), j * HEAD_DIM:(j + 1) * HEAD_DIM]
                ends = [n * CHUNK + (CHUNK - 1 if d == 0 else 0) for n in range(N_SUB)]
                tot = [gcol[e:e + 1, :] for e in ends]
                totcol = jnp.broadcast_to(tot[N_SUB - 1], (SUPER, 1))
                for n in range(N_SUB - 2, -1, -1):
                    totcol = jnp.where(row_blk == n, tot[n], totcol)
                chains.append(dict(
                    d=d, j=j, r0=r0, tot=tot,
                    lmat=jnp.where(strict, _pack_col(bcol) * kk_p * decay, 0.0),
                    rhs=jnp.concatenate([v * bcol, k * (bcol * eg)], axis=1).astype(BF16),
                    qe16=(q * eg).astype(BF16),
                    qkm16=jnp.where(incl, qk_p * decay, 0.0).astype(BF16),
                    kdec16=(k * jnp.exp(totcol - gcol)).astype(BF16),
                    state=s_ref[d * 2 + j],
                    outs=[None] * N_SUB))
        tmats = _unit_tri_inverse_packed([c["lmat"] for c in chains])
        for c, tmat in zip(chains, tmats):
            uw = _dot(_block_diag(tmat.astype(BF16)), c["rhs"])
            c["u"] = uw[:, :HEAD_DIM]
            c["w16"] = uw[:, HEAD_DIM:].astype(BF16)
        for step in range(N_SUB):
            wss = []
            for c in chains:
                n = step if c["d"] == 0 else N_SUB - 1 - step
                lhs = jnp.concatenate([c["w16"][n * CHUNK:(n + 1) * CHUNK],
                                       c["qe16"][n * CHUNK:(n + 1) * CHUNK]], axis=0)
                wss.append(_dot(lhs, c["state"].astype(BF16)))
            for c, ws in zip(chains, wss):
                n = step if c["d"] == 0 else N_SUB - 1 - step
                lo, hi = n * CHUNK, (n + 1) * CHUNK
                vn16 = (c["u"][lo:hi] - ws[:CHUNK]).astype(BF16)
                c["outs"][n] = ws[CHUNK:] + _dot(c["qkm16"][:, lo:hi], vn16)
                c["state"] = c["state"] * jnp.exp(c["tot"][n]) + _dot_tn(c["kdec16"][lo:hi], vn16)
        for c in chains:
            lanes = slice(c["j"] * HEAD_DIM, (c["j"] + 1) * HEAD_DIM)
            o_ref[pl.ds(c["r0"], SUPER), lanes] += jnp.concatenate(c["outs"], axis=0)
            s_ref[c["d"] * 2 + c["j"]] = c["state"]
        return carry

    lax.fori_loop(0, n_super, super_step, 0)

    gain = gain_ref[...]

    def norm_tile(i, carry):
        r0 = pl.multiple_of(i * rows, rows)
        o = o_ref[pl.ds(r0, rows), :]
        z = z_ref[0, pl.ds(r0, rows), :]
        parts = []
        for j in range(2):
            oh = o[:, j * HEAD_DIM:(j + 1) * HEAD_DIM]
            parts.append(oh * _rms_scale(oh) * gain)
        y = jnp.concatenate(parts, axis=-1) * (z * jax.nn.sigmoid(z))
        y_ref[0, pl.ds(r0, rows), :] = y.astype(y_ref.dtype)
        return carry

    lax.fori_loop(0, t // rows, norm_tile, 0)


def _gate_rows(ga, n_kh):
    b, t, _ = ga.shape
    fwd = ga[..., 0:2 * n_kh].reshape(b, t, n_kh, 1, 2)
    bwd = ga[..., 64:64 + 2 * n_kh].reshape(b, t, n_kh, 1, 2)
    g4 = jnp.concatenate([fwd, bwd], axis=3).reshape(b, t // SUPER, SUPER, n_kh, 4)
    g4 = jnp.transpose(g4, (0, 3, 1, 4, 2))
    return jnp.concatenate([g4, jnp.zeros_like(g4)], axis=3)


def _delta(q, k, v, proj, z_col0, ga, out_gain, *, rows=256):
    b, t, kdim = q.shape
    n_kh = kdim // HEAD_DIM
    vdim = v.shape[-1]
    assert vdim == 2 * kdim and t % SUPER == 0 and z_col0 % (2 * HEAD_DIM) == 0 and 2 * n_kh == 32
    rows = min(rows, t)
    n_super = t // SUPER
    z_off = z_col0 // (2 * HEAD_DIM)
    gr = _gate_rows(ga, n_kh)
    return pl.pallas_call(
        functools.partial(_delta_body, n_super=n_super, rows=rows),
        grid=(b, n_kh),
        in_specs=[
            pl.BlockSpec((1, t, HEAD_DIM), lambda i, h: (i, 0, h)),
            pl.BlockSpec((1, t, HEAD_DIM), lambda i, h: (i, 0, h)),
            pl.BlockSpec((1, t, 2 * HEAD_DIM), lambda i, h: (i, 0, h)),
            pl.BlockSpec((1, t, 2 * HEAD_DIM), lambda i, h: (i, 0, h + z_off)),
            pl.BlockSpec((1, t, 128), lambda i, h: (i, 0, 0)),
            pl.BlockSpec((1, 1, n_super, 8, SUPER), lambda i, h: (i, h, 0, 0, 0)),
            pl.BlockSpec((1, HEAD_DIM), lambda i, h: (0, 0)),
        ],
        out_specs=pl.BlockSpec((1, t, 2 * HEAD_DIM), lambda i, h: (i, 0, h)),
        out_shape=jax.ShapeDtypeStruct((b, t, vdim), BF16),
        scratch_shapes=[pltpu.VMEM((t, 2 * HEAD_DIM), F32), pltpu.VMEM((4, HEAD_DIM, HEAD_DIM), F32)],
        compiler_params=_params("parallel", "arbitrary"),
        name="delta",
    )(q, k, v, proj, ga, gr, out_gain.reshape(1, HEAD_DIM).astype(F32))


def _gated_deltanet(x, norm_gain, w_in16, conv_w, a_log, dt_bias, out_norm, w_out16):
    b, s, d = x.shape
    n_gate = 128
    qkvz_dim = w_in16.shape[1] - n_gate
    value_dim = w_out16.shape[0]
    qkv_dim = qkvz_dim - value_dim
    key_dim = (qkv_dim - value_dim) // 2
    x2 = x.reshape(b * s, d)
    proj = _norm_mm(x2, norm_gain, w_in16[:, :qkvz_dim], out_dtype=F32).reshape(b, s, qkvz_dim)
    gates = _norm_mm(x2, norm_gain, w_in16[:, qkvz_dim:], out_dtype=F32).reshape(b, s, n_gate)
    ga = _gates(gates, a_log, dt_bias)
    q = _conv(proj, conv_w[:, :key_dim], col0=0, width=key_dim, l2_scale=HEAD_DIM ** -0.5)
    k = _conv(proj, conv_w[:, key_dim:2 * key_dim], col0=key_dim, width=key_dim, l2_scale=1.0)
    v = _conv(proj, conv_w[:, 2 * key_dim:], col0=2 * key_dim, width=value_dim, l2_scale=None)
    y = _delta(q, k, v, proj, qkv_dim, ga, out_norm)
    return _mm_res(y.reshape(b * s, value_dim), w_out16, x2).reshape(b, s, d)


def _chan_dft_body(x_ref, g_ref, cs_ref, o_ref):
    x = x_ref[0]
    h = (x * _rms_scale(x) * g_ref[...]).astype(BF16)
    cs = cs_ref[...]
    for g in range(x.shape[-1] // FN_GROUP_DIM):
        lo, hi = g * FN_GROUP_DIM, (g + 1) * FN_GROUP_DIM
        pq = _dot(h[:, lo:hi], cs)
        o_ref[0, 0, :, lo:hi] = pq[:, :FN_GROUP_DIM].astype(o_ref.dtype)
        o_ref[0, 1, :, lo:hi] = pq[:, FN_GROUP_DIM:].astype(o_ref.dtype)


def _chan_dft(x, gain, cs, *, tm=512):
    b, s, d = x.shape
    tm = min(tm, s)
    assert s % tm == 0 and d % FN_GROUP_DIM == 0
    return pl.pallas_call(
        _chan_dft_body,
        grid=(b, s // tm),
        in_specs=[
            pl.BlockSpec((1, tm, d), lambda i, j: (i, j, 0)),
            pl.BlockSpec((1, d), lambda i, j: (0, 0)),
            pl.BlockSpec((FN_GROUP_DIM, 2 * FN_GROUP_DIM), lambda i, j: (0, 0)),
        ],
        out_specs=pl.BlockSpec((1, 2, tm, d), lambda i, j: (i, 0, j, 0)),
        out_shape=jax.ShapeDtypeStruct((b, 2, s, d), BF16),
        compiler_params=_params("parallel", "parallel"),
        name="chan_dft",
    )(x, gain.reshape(1, d), cs)


def _seq_dft_body(a_ref, b_ref, o_ref, acc_ref, *, n_k, scale):
    k = pl.program_id(3)
    p = _dot(a_ref[...], b_ref[0])

    @pl.when(k == 0)
    def _():
        acc_ref[...] = p

    @pl.when(k > 0)
    def _():
        acc_ref[...] += p

    @pl.when(k == n_k - 1)
    def _():
        o_ref[0] = (acc_ref[...] * scale).astype(o_ref.dtype)


def _seq_dft(dft, pq, *, scale, tm=2048, tn=1024, tk=1024):
    s, k2 = dft.shape
    b, _, d = pq.shape
    tm, tn, tk = min(tm, s), min(tn, d), min(tk, k2)
    assert s % tm == 0 and d % tn == 0 and k2 % tk == 0
    n_k = k2 // tk
    return pl.pallas_call(
        functools.partial(_seq_dft_body, n_k=n_k, scale=scale),
        grid=(b, s // tm, d // tn, n_k),
        in_specs=[
            pl.BlockSpec((tm, tk), lambda i, m, n, k: (m, k)),
            pl.BlockSpec((1, tk, tn), lambda i, m, n, k: (i, k, n)),
        ],
        out_specs=pl.BlockSpec((1, tm, tn), lambda i, m, n, k: (i, m, n)),
        out_shape=jax.ShapeDtypeStruct((b, s, d), BF16),
        scratch_shapes=[pltpu.VMEM((tm, tn), F32)],
        compiler_params=_params("parallel", "parallel", "parallel", "arbitrary"),
        name="seq_dft",
    )(dft, pq)


def _dft_tables(n):
    idx = jnp.arange(n, dtype=jnp.int32)
    prod = (idx[:, None] * idx[None, :]) % n
    theta = prod.astype(F32) * (2.0 * math.pi / n)
    return jnp.cos(theta), jnp.sin(theta)


def _fourier_mixer(x, norm_gain, w_out16):
    b, s, d = x.shape
    cc, sc = _dft_tables(FN_GROUP_DIM)
    cs_c = jnp.concatenate([cc, sc], axis=1).astype(BF16)
    cs_s, sn_s = _dft_tables(s)
    dft = jnp.concatenate([cs_s, -sn_s], axis=1).astype(BF16)
    pq = _chan_dft(x, norm_gain, cs_c).reshape(b, 2 * s, d)
    mixed = _seq_dft(dft, pq, scale=float((s * FN_GROUP_DIM) ** -0.5))
    return _mm_res(mixed.reshape(b * s, d), w_out16, x.reshape(b * s, d)).reshape(b, s, d)


def kernel(x_prompt, x_sample, ffn1_norm, ffn1_w_gate, ffn1_w_up, ffn1_w_down, mix_norm, dn_w_in, dn_conv_w, dn_a_log, dn_dt_bias, dn_out_norm, dn_w_out, fn_w_out, ffn2_norm, ffn2_w_gate, ffn2_w_up, ffn2_w_down, final_norm):
    depth = ffn1_norm.shape[0]
    n_mixers = 2
    f1 = [(ffn1_w_gate[i].astype(BF16), ffn1_w_up[i].astype(BF16), ffn1_w_down[i].astype(BF16)) for i in range(depth)]
    f2 = [(ffn2_w_gate[i].astype(BF16), ffn2_w_up[i].astype(BF16), ffn2_w_down[i].astype(BF16)) for i in range(depth)]
    dn_w_in16 = dn_w_in.astype(BF16)
    dn_w_out16 = dn_w_out.astype(BF16)
    fn_w_out16 = fn_w_out.astype(BF16)

    def ffn(x, gain, ws, final):
        b, s, d = x.shape
        return _ffn(x.reshape(b * s, d), gain, *ws, final_norm, final_norm=final).reshape(b, s, d)

    def trunk(x):
        for i in range(depth):
            x = ffn(x, ffn1_norm[i], f1[i], False)
            j = i // n_mixers
            if i % n_mixers == 0:
                x = _gated_deltanet(x, mix_norm[i], dn_w_in16[j], dn_conv_w[j], dn_a_log[j], dn_dt_bias[j],
                                    dn_out_norm[j], dn_w_out16[j])
            else:
                x = _fourier_mixer(x, mix_norm[i], fn_w_out16[j])
            x = ffn(x, ffn2_norm[i], f2[i], i == depth - 1)
        return x

    return (trunk(x_prompt), trunk(x_sample))
```

```python
import functools
import math

import jax
import jax.numpy as jnp
from jax import lax
from jax.experimental import pallas as pl
from jax.experimental.pallas import tpu as pltpu

EPS = 1e-6
CHUNK = 64
HEAD_DIM = 128
FN_GROUP_DIM = 256
CONV_WIDTH = 5
CONV_PAD = (CONV_WIDTH - 1) // 2
HALO = 8
VMEM_LIMIT_BYTES = 60 * 1024 * 1024
BF16 = jnp.bfloat16
F32 = jnp.float32
HIGHEST = lax.Precision.HIGHEST


def _params(*semantics):
    return pltpu.CompilerParams(dimension_semantics=semantics, vmem_limit_bytes=VMEM_LIMIT_BYTES)


def _rms_scale(x):
    return lax.rsqrt(jnp.mean(x * x, axis=-1, keepdims=True) + EPS)


def _dot(a, b):
    return jnp.dot(a, b, preferred_element_type=F32)


def _dot_nt(a, b):
    return lax.dot_general(a, b, (((1,), (1,)), ((), ())), preferred_element_type=F32)


def _dot_tn(a, b):
    return lax.dot_general(a, b, (((0,), (0,)), ((), ())), preferred_element_type=F32)


def _ffn_body(x_ref, g_ref, wg_ref, wu_ref, wd_ref, fg_ref, o_ref, h_ref, *, n_f, final_norm):
    f = pl.program_id(1)

    @pl.when(f == 0)
    def _():
        x = x_ref[...]
        h_ref[...] = (x * _rms_scale(x) * g_ref[...]).astype(BF16)
        o_ref[...] = x

    h = h_ref[...]
    a = _dot(h, wg_ref[...])
    b = _dot(h, wu_ref[...])
    act = (a * jax.nn.sigmoid(a) * (b * 0.5)).astype(BF16)
    o_ref[...] += _dot(act, wd_ref[...])

    if final_norm:
        @pl.when(f == n_f - 1)
        def _():
            y = o_ref[...]
            o_ref[...] = y * _rms_scale(y) * fg_ref[...]


def _ffn(x, gain, wg, wu, wd, final_gain, *, final_norm, tm=1024, tf=512):
    m, d = x.shape
    f_dim = wg.shape[1]
    tm = min(tm, m)
    assert m % tm == 0 and f_dim % tf == 0
    n_f = f_dim // tf
    return pl.pallas_call(
        functools.partial(_ffn_body, n_f=n_f, final_norm=final_norm),
        grid=(m // tm, n_f),
        in_specs=[
            pl.BlockSpec((tm, d), lambda i, f: (i, 0)),
            pl.BlockSpec((1, d), lambda i, f: (0, 0)),
            pl.BlockSpec((d, tf), lambda i, f: (0, f)),
            pl.BlockSpec((d, tf), lambda i, f: (0, f)),
            pl.BlockSpec((tf, d), lambda i, f: (f, 0)),
            pl.BlockSpec((1, d), lambda i, f: (0, 0)),
        ],
        out_specs=pl.BlockSpec((tm, d), lambda i, f: (i, 0)),
        out_shape=jax.ShapeDtypeStruct((m, d), F32),
        scratch_shapes=[pltpu.VMEM((tm, d), BF16)],
        compiler_params=_params("parallel", "arbitrary"),
        name="ffn",
    )(x, gain.reshape(1, d), wg, wu, wd, final_gain.reshape(1, d))


def _norm_mm_body(x_ref, g_ref, w_ref, o_ref, h_ref):
    @pl.when(pl.program_id(1) == 0)
    def _():
        x = x_ref[...]
        h_ref[...] = (x * _rms_scale(x) * g_ref[...]).astype(BF16)

    o_ref[...] = _dot(h_ref[...], w_ref[...]).astype(o_ref.dtype)


def _norm_mm(x, gain, w, *, out_dtype, tm=1024, tn=1024):
    m, d = x.shape
    n = w.shape[1]
    tm, tn = min(tm, m), min(tn, n)
    assert m % tm == 0 and n % tn == 0
    return pl.pallas_call(
        _norm_mm_body,
        grid=(m // tm, n // tn),
        in_specs=[
            pl.BlockSpec((tm, d), lambda i, j: (i, 0)),
            pl.BlockSpec((1, d), lambda i, j: (0, 0)),
            pl.BlockSpec((d, tn), lambda i, j: (0, j)),
        ],
        out_specs=pl.BlockSpec((tm, tn), lambda i, j: (i, j)),
        out_shape=jax.ShapeDtypeStruct((m, n), out_dtype),
        scratch_shapes=[pltpu.VMEM((tm, d), BF16)],
        compiler_params=_params("parallel", "arbitrary"),
        name="norm_mm",
    )(x, gain.reshape(1, d), w)


def _mm_res_body(a_ref, w_ref, r_ref, o_ref, acc_ref, *, n_k):
    k = pl.program_id(2)
    p = _dot(a_ref[...], w_ref[...])

    @pl.when(k == 0)
    def _():
        acc_ref[...] = p

    @pl.when(k > 0)
    def _():
        acc_ref[...] += p

    @pl.when(k == n_k - 1)
    def _():
        o_ref[...] = r_ref[...] + acc_ref[...]


def _mm_res(a, w, res, *, tm=512, tn=1024, tk=2048):
    m, kd = a.shape
    n = w.shape[1]
    tm, tn, tk = min(tm, m), min(tn, n), min(tk, kd)
    assert m % tm == 0 and n % tn == 0 and kd % tk == 0
    n_k = kd // tk
    return pl.pallas_call(
        functools.partial(_mm_res_body, n_k=n_k),
        grid=(m // tm, n // tn, n_k),
        in_specs=[
            pl.BlockSpec((tm, tk), lambda i, j, k: (i, k)),
            pl.BlockSpec((tk, tn), lambda i, j, k: (k, j)),
            pl.BlockSpec((tm, tn), lambda i, j, k: (i, j)),
        ],
        out_specs=pl.BlockSpec((tm, tn), lambda i, j, k: (i, j)),
        out_shape=jax.ShapeDtypeStruct((m, n), F32),
        scratch_shapes=[pltpu.VMEM((tm, tn), F32)],
        compiler_params=_params("parallel", "parallel", "arbitrary"),
        name="mm_res",
    )(a, w, res)


def _gates_body(x_ref, alog_ref, dtb_ref, ga_ref, *, n_chunks):
    lane = lax.broadcasted_iota(jnp.int32, (CHUNK, 128), 1)
    is_decay = (lane % 64) < 32
    is_bwd = lane >= 64
    row = lax.broadcasted_iota(jnp.int32, (CHUNK, CHUNK), 0)
    col = lax.broadcasted_iota(jnp.int32, (CHUNK, CHUNK), 1)
    lower = (row >= col).astype(F32)
    upper = (row <= col).astype(F32)
    neg_a = -jnp.exp(alog_ref[...])
    dtb = dtb_ref[...]
    for c in range(n_chunks):
        x = x_ref[0, c * CHUNK:(c + 1) * CHUNK, :]
        t = x + dtb
        softplus = jnp.maximum(t, 0.0) + jnp.log(1.0 + jnp.exp(-jnp.abs(t)))
        g = jnp.where(is_decay, neg_a * softplus, 0.0)
        beta = jax.nn.sigmoid(x)
        cum_f = jnp.dot(lower, g, precision=HIGHEST, preferred_element_type=F32)
        cum_b = jnp.dot(upper, g, precision=HIGHEST, preferred_element_type=F32)
        cum = jnp.where(is_bwd, cum_b, cum_f)
        ga_ref[0, c * CHUNK:(c + 1) * CHUNK, :] = jnp.where(is_decay, cum, beta)


def _gates(gates, a_log, dt_bias, *, tb=512):
    b, t, w = gates.shape
    tb = min(tb, t)
    assert t % tb == 0 and tb % CHUNK == 0 and w == 128
    zeros = jnp.zeros((32,), F32)
    alog_l = jnp.concatenate([a_log[0], zeros, a_log[1], zeros]).reshape(1, 128).astype(F32)
    dtb_l = jnp.concatenate([dt_bias[0], zeros, dt_bias[1], zeros]).reshape(1, 128).astype(F32)
    spec = pl.BlockSpec((1, tb, 128), lambda i, j: (i, j, 0))
    vec = pl.BlockSpec((1, 128), lambda i, j: (0, 0))
    return pl.pallas_call(
        functools.partial(_gates_body, n_chunks=tb // CHUNK),
        grid=(b, t // tb),
        in_specs=[spec, vec, vec],
        out_specs=spec,
        out_shape=jax.ShapeDtypeStruct((b, t, 128), F32),
        compiler_params=_params("parallel", "parallel"),
        name="gates",
    )(gates, alog_l, dtb_l)


def _conv_body(x_ref, w_ref, o_ref, pad_ref, *, t, rows, l2_scale):
    lanes = x_ref.shape[-1]
    zero = jnp.zeros((HALO, lanes), F32)
    pad_ref[0:HALO, :] = zero
    pad_ref[HALO + t:HALO + t + HALO, :] = zero
    pad_ref[HALO:HALO + t, :] = x_ref[0].astype(F32)
    w = w_ref[...]

    def tile(i, carry):
        r0 = pl.multiple_of(i * rows, rows)
        win = pad_ref[pl.ds(r0, rows + 2 * HALO), :]
        acc = None
        for j in range(CONV_WIDTH):
            s = HALO - CONV_PAD + j
            term = win[s:s + rows, :] * w[j:j + 1, :]
            acc = term if acc is None else acc + term
        y = acc * jax.nn.sigmoid(acc)
        if l2_scale is not None:
            parts = []
            for h in range(lanes // HEAD_DIM):
                yh = y[:, h * HEAD_DIM:(h + 1) * HEAD_DIM]
                inv = lax.rsqrt(jnp.sum(yh * yh, axis=-1, keepdims=True) + EPS)
                parts.append(yh * (inv * l2_scale))
            y = jnp.concatenate(parts, axis=-1)
        o_ref[0, pl.ds(r0, rows), :] = y.astype(o_ref.dtype)
        return carry

    lax.fori_loop(0, t // rows, tile, 0)


def _conv(proj, conv_w, *, col0, width, l2_scale, lanes=256, rows=256):
    b, t, _ = proj.shape
    rows = min(rows, t)
    assert width % lanes == 0 and col0 % lanes == 0 and t % rows == 0
    off = col0 // lanes
    w8 = jnp.zeros((8, width), F32).at[:CONV_WIDTH].set(conv_w.astype(F32))
    return pl.pallas_call(
        functools.partial(_conv_body, t=t, rows=rows, l2_scale=l2_scale),
        grid=(b, width // lanes),
        in_specs=[
            pl.BlockSpec((1, t, lanes), lambda i, j: (i, 0, j + off)),
            pl.BlockSpec((8, lanes), lambda i, j: (0, j)),
        ],
        out_specs=pl.BlockSpec((1, t, lanes), lambda i, j: (i, 0, j)),
        out_shape=jax.ShapeDtypeStruct((b, t, width), BF16),
        scratch_shapes=[pltpu.VMEM((t + 2 * HALO, lanes), F32)],
        compiler_params=_params("parallel", "parallel"),
        name="conv",
    )(proj, w8)


N_SUB = 4
SUPER = N_SUB * CHUNK
KH_STEP = 2


def _pack(mat):
    blk = lax.broadcasted_iota(jnp.int32, (CHUNK, SUPER), 1) // CHUNK
    out = mat[(N_SUB - 1) * CHUNK:, :]
    for n in range(N_SUB - 2, -1, -1):
        out = jnp.where(blk == n, mat[n * CHUNK:(n + 1) * CHUNK, :], out)
    return out


def _pack_col(colv):
    blk = lax.broadcasted_iota(jnp.int32, (CHUNK, SUPER), 1) // CHUNK
    out = jnp.broadcast_to(colv[(N_SUB - 1) * CHUNK:, :], (CHUNK, SUPER))
    for n in range(N_SUB - 2, -1, -1):
        out = jnp.where(blk == n, jnp.broadcast_to(colv[n * CHUNK:(n + 1) * CHUNK, :], (CHUNK, SUPER)), out)
    return out


def _block_diag(packed):
    r = lax.broadcasted_iota(jnp.int32, (SUPER, SUPER), 0) // CHUNK
    c = lax.broadcasted_iota(jnp.int32, (SUPER, SUPER), 1) // CHUNK
    return jnp.where(r == c, jnp.concatenate([packed] * N_SUB, axis=0), jnp.zeros((), packed.dtype))


def _unit_tri_inverse_packed(lps):
    i = lax.broadcasted_iota(jnp.int32, (CHUNK, SUPER), 0)
    j = lax.broadcasted_iota(jnp.int32, (CHUNK, SUPER), 1) % CHUNK
    eye = jnp.where(i == j, 1.0, 0.0)
    xs = [-lp for lp in lps]
    ps = [eye + x for x in xs]
    n_levels = int(math.log2(CHUNK)) - 1
    xs = [_dot(x.astype(BF16), _block_diag(x.astype(BF16))) for x in xs]
    for lvl in range(n_levels):
        rhss = [_block_diag(x.astype(BF16)) for x in xs]
        if lvl == n_levels - 1:
            ps = [p + _dot(p.astype(BF16), rhs) for p, rhs in zip(ps, rhss)]
        else:
            boths = [_dot(jnp.concatenate([x, p], axis=0).astype(BF16), rhs) for x, p, rhs in zip(xs, ps, rhss)]
            xs = [both[:CHUNK] for both in boths]
            ps = [p + both[CHUNK:] for p, both in zip(ps, boths)]
    return ps


def _delta_body(q_ref, k_ref, v_ref, z_ref, ga_ref, gr_ref, gain_ref, y_ref, o_ref, s_ref,
                *, n_super, rows):
    kh0 = pl.program_id(1) * KH_STEP
    t = n_super * SUPER
    o_ref[...] = jnp.zeros_like(o_ref)
    s_ref[...] = jnp.zeros_like(s_ref)

    pi = lax.broadcasted_iota(jnp.int32, (CHUNK, SUPER), 0)
    pj = lax.broadcasted_iota(jnp.int32, (CHUNK, SUPER), 1) % CHUNK
    gate_lane = lax.broadcasted_iota(jnp.int32, (SUPER, 128), 1)
    row_blk = lax.broadcasted_iota(jnp.int32, (SUPER, 1), 0) // CHUNK
    col_blk = lax.broadcasted_iota(jnp.int32, (HEAD_DIM, SUPER), 1) // CHUNK

    def super_step(s, carry):
        chains = []
        for d in range(2):
            ss = s if d == 0 else n_super - 1 - s
            r0 = pl.multiple_of(ss * SUPER, SUPER)
            incl = (pi >= pj) if d == 0 else (pi <= pj)
            strict = (pi > pj) if d == 0 else (pi < pj)
            ga = ga_ref[0, pl.ds(r0, SUPER), :]
            for kk in range(KH_STEP):
                q16 = q_ref[0, pl.ds(r0, SUPER), kk * HEAD_DIM:(kk + 1) * HEAD_DIM]
                k16 = k_ref[0, pl.ds(r0, SUPER), kk * HEAD_DIM:(kk + 1) * HEAD_DIM]
                grows = gr_ref[0, kk, ss]
                gram = _dot_nt(jnp.concatenate([k16, q16], axis=0), k16)
                kk_p = _pack(gram[:SUPER])
                qk_p = _pack(gram[SUPER:])
                q = q16.astype(F32)
                k = k16.astype(F32)
                for j in range(2):
                    hv = 2 * kk + j
                    lane_g = d * 64 + 2 * kh0 + hv
                    gcol = jnp.sum(jnp.where(gate_lane == lane_g, ga, 0.0), axis=-1, keepdims=True)
                    bcol = jnp.sum(jnp.where(gate_lane == lane_g + 32, ga, 0.0), axis=-1, keepdims=True)
                    grow = grows[d * 2 + j:d * 2 + j + 1, :]
                    diff = _pack_col(gcol) - grow
                    decay = jnp.where(incl, jnp.exp(jnp.where(incl, diff, 0.0)), 0.0)
                    eg = jnp.exp(gcol)
                    v = v_ref[0, pl.ds(r0, SUPER), hv * HEAD_DIM:(hv + 1) * HEAD_DIM].astype(F32)
                    ends = [n * CHUNK + (CHUNK - 1 if d == 0 else 0) for n in range(N_SUB)]
                    tot = [gcol[e:e + 1, :] for e in ends]
                    totcol = jnp.broadcast_to(tot[N_SUB - 1], (SUPER, 1))
                    for n in range(N_SUB - 2, -1, -1):
                        totcol = jnp.where(row_blk == n, tot[n], totcol)
                    kd_t = jnp.transpose(k * jnp.exp(totcol - gcol))
                    qkm_bd = _block_diag(jnp.where(incl, qk_p * decay, 0.0))
                    pre_lhs = jnp.concatenate(
                        [jnp.where(col_blk == n, kd_t, 0.0) for n in range(N_SUB)] + [qkm_bd], axis=0).astype(BF16)
                    chains.append(dict(
                        d=d, hv=hv, r0=r0, tot=tot, pre_lhs=pre_lhs,
                        lmat=jnp.where(strict, _pack_col(bcol) * kk_p * decay, 0.0),
                        rhs=jnp.concatenate([v * bcol, k * (bcol * eg)], axis=1).astype(BF16),
                        qe=q * eg,
                        state=s_ref[d * 2 * KH_STEP + hv],
                        outs=[None] * N_SUB))
        tmats = _unit_tri_inverse_packed([c["lmat"] for c in chains])
        uws = [_dot(_block_diag(tmat.astype(BF16)), c["rhs"]).astype(BF16) for c, tmat in zip(chains, tmats)]
        pres = [_dot(c["pre_lhs"], uw) for c, uw in zip(chains, uws)]
        for c, pre in zip(chains, pres):
            c["b"] = [pre[n * HEAD_DIM:(n + 1) * HEAD_DIM, :HEAD_DIM] for n in range(N_SUB)]
            qw = pre[N_SUB * HEAD_DIM:, HEAD_DIM:]
            c["o0"] = pre[N_SUB * HEAD_DIM:, :HEAD_DIM]
            c["lhs"] = [jnp.concatenate([-pre[n * HEAD_DIM:(n + 1) * HEAD_DIM, HEAD_DIM:],
                                         c["qe"][n * CHUNK:(n + 1) * CHUNK] - qw[n * CHUNK:(n + 1) * CHUNK]],
                                        axis=0).astype(BF16) for n in range(N_SUB)]
        for step in range(N_SUB):
            order = [step if c["d"] == 0 else N_SUB - 1 - step for c in chains]
            prods = [_dot(c["lhs"][n], c["state"].astype(BF16)) for c, n in zip(chains, order)]
            for c, n, prod in zip(chains, order, prods):
                c["outs"][n] = prod[HEAD_DIM:] + c["o0"][n * CHUNK:(n + 1) * CHUNK]
                c["state"] = c["state"] * jnp.exp(c["tot"][n]) + (prod[:HEAD_DIM] + c["b"][n])
        for c in chains:
            lanes = slice(c["hv"] * HEAD_DIM, (c["hv"] + 1) * HEAD_DIM)
            o_ref[pl.ds(c["r0"], SUPER), lanes] += jnp.concatenate(c["outs"], axis=0)
            s_ref[c["d"] * 2 * KH_STEP + c["hv"]] = c["state"]
        return carry

    lax.fori_loop(0, n_super, super_step, 0)

    gain = gain_ref[...]

    def norm_tile(i, carry):
        r0 = pl.multiple_of(i * rows, rows)
        o = o_ref[pl.ds(r0, rows), :]
        z = z_ref[0, pl.ds(r0, rows), :].astype(F32)
        parts = []
        for hv in range(2 * KH_STEP):
            oh = o[:, hv * HEAD_DIM:(hv + 1) * HEAD_DIM]
            parts.append(oh * _rms_scale(oh) * gain)
        y = jnp.concatenate(parts, axis=-1) * (z * jax.nn.sigmoid(z))
        y_ref[0, pl.ds(r0, rows), :] = y.astype(y_ref.dtype)
        return carry

    lax.fori_loop(0, t // rows, norm_tile, 0)


def _gate_rows(ga, n_kh):
    b, t, _ = ga.shape
    fwd = ga[..., 0:2 * n_kh].reshape(b, t, n_kh, 1, 2)
    bwd = ga[..., 64:64 + 2 * n_kh].reshape(b, t, n_kh, 1, 2)
    g4 = jnp.concatenate([fwd, bwd], axis=3).reshape(b, t // SUPER, SUPER, n_kh, 4)
    g4 = jnp.transpose(g4, (0, 3, 1, 4, 2))
    return jnp.concatenate([g4, jnp.zeros_like(g4)], axis=3)


def _delta(q, k, v, proj, z_col0, ga, out_gain, *, rows=256):
    b, t, kdim = q.shape
    n_kh = kdim // HEAD_DIM
    vdim = v.shape[-1]
    kw, vw = KH_STEP * HEAD_DIM, 2 * KH_STEP * HEAD_DIM
    assert vdim == 2 * kdim and t % SUPER == 0 and z_col0 % vw == 0 and 2 * n_kh == 32 and n_kh % KH_STEP == 0
    rows = min(rows, t)
    n_super = t // SUPER
    z_off = z_col0 // vw
    gr = _gate_rows(ga, n_kh)
    return pl.pallas_call(
        functools.partial(_delta_body, n_super=n_super, rows=rows),
        grid=(b, n_kh // KH_STEP),
        in_specs=[
            pl.BlockSpec((1, t, kw), lambda i, h: (i, 0, h)),
            pl.BlockSpec((1, t, kw), lambda i, h: (i, 0, h)),
            pl.BlockSpec((1, t, vw), lambda i, h: (i, 0, h)),
            pl.BlockSpec((1, t, vw), lambda i, h: (i, 0, h + z_off)),
            pl.BlockSpec((1, t, 128), lambda i, h: (i, 0, 0)),
            pl.BlockSpec((1, KH_STEP, n_super, 8, SUPER), lambda i, h: (i, h, 0, 0, 0)),
            pl.BlockSpec((1, HEAD_DIM), lambda i, h: (0, 0)),
        ],
        out_specs=pl.BlockSpec((1, t, vw), lambda i, h: (i, 0, h)),
        out_shape=jax.ShapeDtypeStruct((b, t, vdim), BF16),
        scratch_shapes=[pltpu.VMEM((t, vw), F32), pltpu.VMEM((4 * KH_STEP, HEAD_DIM, HEAD_DIM), F32)],
        compiler_params=_params("parallel", "arbitrary"),
        name="delta",
    )(q, k, v, proj, ga, gr, out_gain.reshape(1, HEAD_DIM).astype(F32))


def _gated_deltanet(x, norm_gain, w_in16, conv_w, a_log, dt_bias, out_norm, w_out16):
    b, s, d = x.shape
    n_gate = 128
    qkvz_dim = w_in16.shape[1] - n_gate
    value_dim = w_out16.shape[0]
    qkv_dim = qkvz_dim - value_dim
    key_dim = (qkv_dim - value_dim) // 2
    x2 = x.reshape(b * s, d)
    proj = _norm_mm(x2, norm_gain, w_in16[:, :qkvz_dim], out_dtype=BF16).reshape(b, s, qkvz_dim)
    gates = _norm_mm(x2, norm_gain, w_in16[:, qkvz_dim:], out_dtype=F32).reshape(b, s, n_gate)
    ga = _gates(gates, a_log, dt_bias)
    q = _conv(proj, conv_w[:, :key_dim], col0=0, width=key_dim, l2_scale=HEAD_DIM ** -0.5)
    k = _conv(proj, conv_w[:, key_dim:2 * key_dim], col0=key_dim, width=key_dim, l2_scale=1.0)
    v = _conv(proj, conv_w[:, 2 * key_dim:], col0=2 * key_dim, width=value_dim, l2_scale=None)
    y = _delta(q, k, v, proj, qkv_dim, ga, out_norm)
    return _mm_res(y.reshape(b * s, value_dim), w_out16, x2).reshape(b, s, d)


def _chan_dft_body(x_ref, g_ref, cs_ref, o_ref):
    x = x_ref[0]
    h = (x * _rms_scale(x) * g_ref[...]).astype(BF16)
    cs = cs_ref[...]
    for g in range(x.shape[-1] // FN_GROUP_DIM):
        lo, hi = g * FN_GROUP_DIM, (g + 1) * FN_GROUP_DIM
        pq = _dot(h[:, lo:hi], cs)
        o_ref[0, 0, :, lo:hi] = pq[:, :FN_GROUP_DIM].astype(o_ref.dtype)
        o_ref[0, 1, :, lo:hi] = pq[:, FN_GROUP_DIM:].astype(o_ref.dtype)


def _chan_dft(x, gain, cs, *, tm=512):
    b, s, d = x.shape
    tm = min(tm, s)
    assert s % tm == 0 and d % FN_GROUP_DIM == 0
    return pl.pallas_call(
        _chan_dft_body,
        grid=(b, s // tm),
        in_specs=[
            pl.BlockSpec((1, tm, d), lambda i, j: (i, j, 0)),
            pl.BlockSpec((1, d), lambda i, j: (0, 0)),
            pl.BlockSpec((FN_GROUP_DIM, 2 * FN_GROUP_DIM), lambda i, j: (0, 0)),
        ],
        out_specs=pl.BlockSpec((1, 2, tm, d), lambda i, j: (i, 0, j, 0)),
        out_shape=jax.ShapeDtypeStruct((b, 2, s, d), BF16),
        compiler_params=_params("parallel", "parallel"),
        name="chan_dft",
    )(x, gain.reshape(1, d), cs)


def _seq_dft_body(a_ref, b_ref, o_ref, acc_ref, *, n_k, scale):
    k = pl.program_id(3)
    p = _dot(a_ref[...], b_ref[0])

    @pl.when(k == 0)
    def _():
        acc_ref[...] = p

    @pl.when(k > 0)
    def _():
        acc_ref[...] += p

    @pl.when(k == n_k - 1)
    def _():
        o_ref[0] = (acc_ref[...] * scale).astype(o_ref.dtype)


def _seq_dft(dft, pq, *, scale, tm=2048, tn=1024, tk=1024):
    s, k2 = dft.shape
    b, _, d = pq.shape
    tm, tn, tk = min(tm, s), min(tn, d), min(tk, k2)
    assert s % tm == 0 and d % tn == 0 and k2 % tk == 0
    n_k = k2 // tk
    return pl.pallas_call(
        functools.partial(_seq_dft_body, n_k=n_k, scale=scale),
        grid=(b, s // tm, d // tn, n_k),
        in_specs=[
            pl.BlockSpec((tm, tk), lambda i, m, n, k: (m, k)),
            pl.BlockSpec((1, tk, tn), lambda i, m, n, k: (i, k, n)),
        ],
        out_specs=pl.BlockSpec((1, tm, tn), lambda i, m, n, k: (i, m, n)),
        out_shape=jax.ShapeDtypeStruct((b, s, d), BF16),
        scratch_shapes=[pltpu.VMEM((tm, tn), F32)],
        compiler_params=_params("parallel", "parallel", "parallel", "arbitrary"),
        name="seq_dft",
    )(dft, pq)


def _dft_tables(n):
    idx = jnp.arange(n, dtype=jnp.int32)
    prod = (idx[:, None] * idx[None, :]) % n
    theta = prod.astype(F32) * (2.0 * math.pi / n)
    return jnp.cos(theta), jnp.sin(theta)


def _fourier_mixer(x, norm_gain, w_out16):
    b, s, d = x.shape
    cc, sc = _dft_tables(FN_GROUP_DIM)
    cs_c = jnp.concatenate([cc, sc], axis=1).astype(BF16)
    cs_s, sn_s = _dft_tables(s)
    dft = jnp.concatenate([cs_s, -sn_s], axis=1).astype(BF16)
    pq = _chan_dft(x, norm_gain, cs_c).reshape(b, 2 * s, d)
    mixed = _seq_dft(dft, pq, scale=float((s * FN_GROUP_DIM) ** -0.5))
    return _mm_res(mixed.reshape(b * s, d), w_out16, x.reshape(b * s, d)).reshape(b, s, d)


def kernel(x_prompt, x_sample, ffn1_norm, ffn1_w_gate, ffn1_w_up, ffn1_w_down, mix_norm, dn_w_in, dn_conv_w, dn_a_log, dn_dt_bias, dn_out_norm, dn_w_out, fn_w_out, ffn2_norm, ffn2_w_gate, ffn2_w_up, ffn2_w_down, final_norm):
    depth = ffn1_norm.shape[0]
    n_mixers = 2
    f1 = [(ffn1_w_gate[i].astype(BF16), ffn1_w_up[i].astype(BF16), ffn1_w_down[i].astype(BF16)) for i in range(depth)]
    f2 = [(ffn2_w_gate[i].astype(BF16), ffn2_w_up[i].astype(BF16), ffn2_w_down[i].astype(BF16)) for i in range(depth)]
    dn_w_in16 = dn_w_in.astype(BF16)
    dn_w_out16 = dn_w_out.astype(BF16)
    fn_w_out16 = fn_w_out.astype(BF16)

    def ffn(x, gain, ws, final):
        b, s, d = x.shape
        return _ffn(x.reshape(b * s, d), gain, *ws, final_norm, final_norm=final).reshape(b, s, d)

    def trunk(x):
        for i in range(depth):
            x = ffn(x, ffn1_norm[i], f1[i], False)
            j = i // n_mixers
            if i % n_mixers == 0:
                x = _gated_deltanet(x, mix_norm[i], dn_w_in16[j], dn_conv_w[j], dn_a_log[j], dn_dt_bias[j],
                                    dn_out_norm[j], dn_w_out16[j])
            else:
                x = _fourier_mixer(x, mix_norm[i], fn_w_out16[j])
            x = ffn(x, ffn2_norm[i], f2[i], i == depth - 1)
        return x

    return (trunk(x_prompt), trunk(x_sample))
```

```python
import functools
import math

import jax
import jax.numpy as jnp
from jax import lax
from jax.experimental import pallas as pl
from jax.experimental.pallas import tpu as pltpu

EPS = 1e-6
CHUNK = 64
HEAD_DIM = 128
FN_GROUP_DIM = 256
CONV_WIDTH = 5
CONV_PAD = (CONV_WIDTH - 1) // 2
HALO = 8
VMEM_LIMIT_BYTES = 60 * 1024 * 1024
BF16 = jnp.bfloat16
F32 = jnp.float32
HIGHEST = lax.Precision.HIGHEST


def _params(*semantics):
    return pltpu.CompilerParams(dimension_semantics=semantics, vmem_limit_bytes=VMEM_LIMIT_BYTES)


def _rms_scale(x):
    return lax.rsqrt(jnp.mean(x * x, axis=-1, keepdims=True) + EPS)


def _dot(a, b):
    return jnp.dot(a, b, preferred_element_type=F32)


def _dot_nt(a, b):
    return lax.dot_general(a, b, (((1,), (1,)), ((), ())), preferred_element_type=F32)


def _ffn_body(x_ref, g_ref, wg_ref, wu_ref, wd_ref, fg_ref, o_ref, h_ref, *, n_f, final_norm):
    f = pl.program_id(1)

    @pl.when(f == 0)
    def _():
        x = x_ref[...]
        h_ref[...] = (x * _rms_scale(x) * g_ref[...]).astype(BF16)
        o_ref[...] = x

    h = h_ref[...]
    half = wg_ref.shape[1] // 2
    gates = [_dot(h, wg_ref[:, c * half:(c + 1) * half]) for c in range(2)]
    ups = [_dot(h, wu_ref[:, c * half:(c + 1) * half]) for c in range(2)]
    acts = [(a * jax.nn.sigmoid(a) * (b * 0.5)).astype(BF16) for a, b in zip(gates, ups)]
    o_ref[...] += _dot(acts[0], wd_ref[:half, :]) + _dot(acts[1], wd_ref[half:, :])

    if final_norm:
        @pl.when(f == n_f - 1)
        def _():
            y = o_ref[...]
            o_ref[...] = y * _rms_scale(y) * fg_ref[...]


def _ffn(x, gain, wg, wu, wd, final_gain, *, final_norm, tm=1024, tf=512):
    m, d = x.shape
    f_dim = wg.shape[1]
    tm = min(tm, m)
    assert m % tm == 0 and f_dim % tf == 0
    n_f = f_dim // tf
    return pl.pallas_call(
        functools.partial(_ffn_body, n_f=n_f, final_norm=final_norm),
        grid=(m // tm, n_f),
        in_specs=[
            pl.BlockSpec((tm, d), lambda i, f: (i, 0)),
            pl.BlockSpec((1, d), lambda i, f: (0, 0)),
            pl.BlockSpec((d, tf), lambda i, f: (0, f)),
            pl.BlockSpec((d, tf), lambda i, f: (0, f)),
            pl.BlockSpec((tf, d), lambda i, f: (f, 0)),
            pl.BlockSpec((1, d), lambda i, f: (0, 0)),
        ],
        out_specs=pl.BlockSpec((tm, d), lambda i, f: (i, 0)),
        out_shape=jax.ShapeDtypeStruct((m, d), F32),
        scratch_shapes=[pltpu.VMEM((tm, d), BF16)],
        compiler_params=_params("parallel", "arbitrary"),
        name="ffn",
    )(x, gain.reshape(1, d), wg, wu, wd, final_gain.reshape(1, d))


def _norm_mm_body(x_ref, g_ref, w_ref, o_ref, h_ref):
    @pl.when(pl.program_id(1) == 0)
    def _():
        x = x_ref[...]
        h_ref[...] = (x * _rms_scale(x) * g_ref[...]).astype(BF16)

    o_ref[...] = _dot(h_ref[...], w_ref[...]).astype(o_ref.dtype)


def _norm_mm(x, gain, w, *, out_dtype, tm=1024, tn=1024):
    m, d = x.shape
    n = w.shape[1]
    tm, tn = min(tm, m), min(tn, n)
    assert m % tm == 0 and n % tn == 0
    return pl.pallas_call(
        _norm_mm_body,
        grid=(m // tm, n // tn),
        in_specs=[
            pl.BlockSpec((tm, d), lambda i, j: (i, 0)),
            pl.BlockSpec((1, d), lambda i, j: (0, 0)),
            pl.BlockSpec((d, tn), lambda i, j: (0, j)),
        ],
        out_specs=pl.BlockSpec((tm, tn), lambda i, j: (i, j)),
        out_shape=jax.ShapeDtypeStruct((m, n), out_dtype),
        scratch_shapes=[pltpu.VMEM((tm, d), BF16)],
        compiler_params=_params("parallel", "arbitrary"),
        name="norm_mm",
    )(x, gain.reshape(1, d), w)


def _mm_res_body(a_ref, w_ref, r_ref, o_ref):
    k = pl.program_id(2)
    p = _dot(a_ref[...], w_ref[...])

    @pl.when(k == 0)
    def _():
        o_ref[...] = r_ref[...] + p

    @pl.when(k > 0)
    def _():
        o_ref[...] += p


def _mm_res(a, w, res, *, tm=1024, tn=2048, tk=1024):
    m, kd = a.shape
    n = w.shape[1]
    tm, tn, tk = min(tm, m), min(tn, n), min(tk, kd)
    assert m % tm == 0 and n % tn == 0 and kd % tk == 0
    return pl.pallas_call(
        _mm_res_body,
        grid=(m // tm, n // tn, kd // tk),
        in_specs=[
            pl.BlockSpec((tm, tk), lambda i, j, k: (i, k)),
            pl.BlockSpec((tk, tn), lambda i, j, k: (k, j)),
            pl.BlockSpec((tm, tn), lambda i, j, k: (i, j)),
        ],
        out_specs=pl.BlockSpec((tm, tn), lambda i, j, k: (i, j)),
        out_shape=jax.ShapeDtypeStruct((m, n), F32),
        compiler_params=_params("parallel", "parallel", "arbitrary"),
        name="mm_res",
    )(a, w, res)


def _gates_body(x_ref, alog_ref, dtb_ref, ga_ref, *, n_chunks):
    lane = lax.broadcasted_iota(jnp.int32, (CHUNK, 128), 1)
    is_decay = (lane % 64) < 32
    is_bwd = lane >= 64
    row = lax.broadcasted_iota(jnp.int32, (CHUNK, CHUNK), 0)
    col = lax.broadcasted_iota(jnp.int32, (CHUNK, CHUNK), 1)
    lower = (row >= col).astype(F32)
    upper = (row <= col).astype(F32)
    neg_a = -jnp.exp(alog_ref[...])
    dtb = dtb_ref[...]
    for c in range(n_chunks):
        x = x_ref[0, c * CHUNK:(c + 1) * CHUNK, :]
        t = x + dtb
        softplus = jnp.maximum(t, 0.0) + jnp.log(1.0 + jnp.exp(-jnp.abs(t)))
        g = jnp.where(is_decay, neg_a * softplus, 0.0)
        beta = jax.nn.sigmoid(x)
        cum_f = jnp.dot(lower, g, precision=HIGHEST, preferred_element_type=F32)
        cum_b = jnp.dot(upper, g, precision=HIGHEST, preferred_element_type=F32)
        cum = jnp.where(is_bwd, cum_b, cum_f)
        ga_ref[0, c * CHUNK:(c + 1) * CHUNK, :] = jnp.where(is_decay, cum, beta)


def _gates(gates, a_log, dt_bias, *, tb=512):
    b, t, w = gates.shape
    tb = min(tb, t)
    assert t % tb == 0 and tb % CHUNK == 0 and w == 128
    zeros = jnp.zeros((32,), F32)
    alog_l = jnp.concatenate([a_log[0], zeros, a_log[1], zeros]).reshape(1, 128).astype(F32)
    dtb_l = jnp.concatenate([dt_bias[0], zeros, dt_bias[1], zeros]).reshape(1, 128).astype(F32)
    spec = pl.BlockSpec((1, tb, 128), lambda i, j: (i, j, 0))
    vec = pl.BlockSpec((1, 128), lambda i, j: (0, 0))
    return pl.pallas_call(
        functools.partial(_gates_body, n_chunks=tb // CHUNK),
        grid=(b, t // tb),
        in_specs=[spec, vec, vec],
        out_specs=spec,
        out_shape=jax.ShapeDtypeStruct((b, t, 128), F32),
        compiler_params=_params("parallel", "parallel"),
        name="gates",
    )(gates, alog_l, dtb_l)


def _conv_body(x_ref, w_ref, o_ref, pad_ref, *, t, rows, l2_scale):
    lanes = x_ref.shape[-1]
    zero = jnp.zeros((HALO, lanes), F32)
    pad_ref[0:HALO, :] = zero
    pad_ref[HALO + t:HALO + t + HALO, :] = zero
    pad_ref[HALO:HALO + t, :] = x_ref[0].astype(F32)
    w = w_ref[...]

    def tile(i, carry):
        r0 = pl.multiple_of(i * rows, rows)
        win = pad_ref[pl.ds(r0, rows + 2 * HALO), :]
        acc = None
        for j in range(CONV_WIDTH):
            s = HALO - CONV_PAD + j
            term = win[s:s + rows, :] * w[j:j + 1, :]
            acc = term if acc is None else acc + term
        y = acc * jax.nn.sigmoid(acc)
        if l2_scale is not None:
            parts = []
            for h in range(lanes // HEAD_DIM):
                yh = y[:, h * HEAD_DIM:(h + 1) * HEAD_DIM]
                inv = lax.rsqrt(jnp.sum(yh * yh, axis=-1, keepdims=True) + EPS)
                parts.append(yh * (inv * l2_scale))
            y = jnp.concatenate(parts, axis=-1)
        o_ref[0, pl.ds(r0, rows), :] = y.astype(o_ref.dtype)
        return carry

    lax.fori_loop(0, t // rows, tile, 0)


def _conv(proj, conv_w, *, col0, width, l2_scale, lanes=256, rows=256):
    b, t, _ = proj.shape
    rows = min(rows, t)
    assert width % lanes == 0 and col0 % lanes == 0 and t % rows == 0
    off = col0 // lanes
    w8 = jnp.zeros((8, width), F32).at[:CONV_WIDTH].set(conv_w.astype(F32))
    return pl.pallas_call(
        functools.partial(_conv_body, t=t, rows=rows, l2_scale=l2_scale),
        grid=(b, width // lanes),
        in_specs=[
            pl.BlockSpec((1, t, lanes), lambda i, j: (i, 0, j + off)),
            pl.BlockSpec((8, lanes), lambda i, j: (0, j)),
        ],
        out_specs=pl.BlockSpec((1, t, lanes), lambda i, j: (i, 0, j)),
        out_shape=jax.ShapeDtypeStruct((b, t, width), BF16),
        scratch_shapes=[pltpu.VMEM((t + 2 * HALO, lanes), F32)],
        compiler_params=_params("parallel", "parallel"),
        name="conv",
    )(proj, w8)


N_SUB = 4
SUPER = N_SUB * CHUNK
KH_STEP = 2


def _pack(mat):
    blk = lax.broadcasted_iota(jnp.int32, (CHUNK, SUPER), 1) // CHUNK
    out = mat[(N_SUB - 1) * CHUNK:, :]
    for n in range(N_SUB - 2, -1, -1):
        out = jnp.where(blk == n, mat[n * CHUNK:(n + 1) * CHUNK, :], out)
    return out


def _pack_col(colv):
    blk = lax.broadcasted_iota(jnp.int32, (CHUNK, SUPER), 1) // CHUNK
    out = jnp.broadcast_to(colv[(N_SUB - 1) * CHUNK:, :], (CHUNK, SUPER))
    for n in range(N_SUB - 2, -1, -1):
        out = jnp.where(blk == n, jnp.broadcast_to(colv[n * CHUNK:(n + 1) * CHUNK, :], (CHUNK, SUPER)), out)
    return out


def _block_diag(packed):
    r = lax.broadcasted_iota(jnp.int32, (SUPER, SUPER), 0) // CHUNK
    c = lax.broadcasted_iota(jnp.int32, (SUPER, SUPER), 1) // CHUNK
    return jnp.where(r == c, jnp.concatenate([packed] * N_SUB, axis=0), jnp.zeros((), packed.dtype))


def _unit_tri_inverse_packed(lps):
    i = lax.broadcasted_iota(jnp.int32, (CHUNK, SUPER), 0)
    j = lax.broadcasted_iota(jnp.int32, (CHUNK, SUPER), 1) % CHUNK
    eye = jnp.where(i == j, 1.0, 0.0)
    xs = [-lp for lp in lps]
    ps = [eye + x for x in xs]
    n_levels = int(math.log2(CHUNK)) - 1
    xs = [_dot(x.astype(BF16), _block_diag(x.astype(BF16))) for x in xs]
    for lvl in range(n_levels):
        rhss = [_block_diag(x.astype(BF16)) for x in xs]
        if lvl == n_levels - 1:
            ps = [p + _dot(p.astype(BF16), rhs) for p, rhs in zip(ps, rhss)]
        else:
            boths = [_dot(jnp.concatenate([x, p], axis=0).astype(BF16), rhs) for x, p, rhs in zip(xs, ps, rhss)]
            xs = [both[:CHUNK] for both in boths]
            ps = [p + both[CHUNK:] for p, both in zip(ps, boths)]
    return ps


def _delta_body(q_ref, k_ref, v_ref, z_ref, ga_ref, gr_ref, gain_ref, y_ref, o_ref, s_ref,
                *, n_super, rows):
    kh0 = pl.program_id(1) * KH_STEP
    t = n_super * SUPER
    o_ref[...] = jnp.zeros_like(o_ref)
    s_ref[...] = jnp.zeros_like(s_ref)

    pi = lax.broadcasted_iota(jnp.int32, (CHUNK, SUPER), 0)
    pj = lax.broadcasted_iota(jnp.int32, (CHUNK, SUPER), 1) % CHUNK
    gate_lane = lax.broadcasted_iota(jnp.int32, (SUPER, 128), 1)
    row_blk = lax.broadcasted_iota(jnp.int32, (SUPER, 1), 0) // CHUNK
    col_blk = lax.broadcasted_iota(jnp.int32, (HEAD_DIM, SUPER), 1) // CHUNK

    def super_step(s, carry):
        chains = []
        for d in range(2):
            ss = s if d == 0 else n_super - 1 - s
            r0 = pl.multiple_of(ss * SUPER, SUPER)
            incl = (pi >= pj) if d == 0 else (pi <= pj)
            strict = (pi > pj) if d == 0 else (pi < pj)
            ga = ga_ref[0, pl.ds(r0, SUPER), :]
            for kk in range(KH_STEP):
                q16 = q_ref[0, pl.ds(r0, SUPER), kk * HEAD_DIM:(kk + 1) * HEAD_DIM]
                k16 = k_ref[0, pl.ds(r0, SUPER), kk * HEAD_DIM:(kk + 1) * HEAD_DIM]
                grows = gr_ref[0, kk, ss]
                gram = _dot_nt(jnp.concatenate([k16, q16], axis=0), k16)
                kk_p = _pack(gram[:SUPER])
                qk_p = _pack(gram[SUPER:])
                q = q16.astype(F32)
                k = k16.astype(F32)
                for j in range(2):
                    hv = 2 * kk + j
                    lane_g = d * 64 + 2 * kh0 + hv
                    gcol = jnp.sum(jnp.where(gate_lane == lane_g, ga, 0.0), axis=-1, keepdims=True)
                    bcol = jnp.sum(jnp.where(gate_lane == lane_g + 32, ga, 0.0), axis=-1, keepdims=True)
                    grow = grows[d * 2 + j:d * 2 + j + 1, :]
                    diff = _pack_col(gcol) - grow
                    decay = jnp.where(incl, jnp.exp(jnp.where(incl, diff, 0.0)), 0.0)
                    eg = jnp.exp(gcol)
                    v = v_ref[0, pl.ds(r0, SUPER), hv * HEAD_DIM:(hv + 1) * HEAD_DIM].astype(F32)
                    ends = [n * CHUNK + (CHUNK - 1 if d == 0 else 0) for n in range(N_SUB)]
                    tot = [gcol[e:e + 1, :] for e in ends]
                    totcol = jnp.broadcast_to(tot[N_SUB - 1], (SUPER, 1))
                    for n in range(N_SUB - 2, -1, -1):
                        totcol = jnp.where(row_blk == n, tot[n], totcol)
                    kd_t = jnp.transpose(k * jnp.exp(totcol - gcol))
                    qkm_bd = _block_diag(jnp.where(incl, qk_p * decay, 0.0))
                    pre_lhs = jnp.concatenate(
                        [jnp.where(col_blk == n, kd_t, 0.0) for n in range(N_SUB)] + [qkm_bd], axis=0).astype(BF16)
                    chains.append(dict(
                        d=d, hv=hv, r0=r0, tot=tot, pre_lhs=pre_lhs,
                        lmat=jnp.where(strict, _pack_col(bcol) * kk_p * decay, 0.0),
                        rhs=jnp.concatenate([v * bcol, k * (bcol * eg)], axis=1).astype(BF16),
                        qe=q * eg,
                        state=s_ref[d * 2 * KH_STEP + hv],
                        outs=[None] * N_SUB))
        tmats = _unit_tri_inverse_packed([c["lmat"] for c in chains])
        uws = [_dot(_block_diag(tmat.astype(BF16)), c["rhs"]).astype(BF16) for c, tmat in zip(chains, tmats)]
        pres = [_dot(c["pre_lhs"], uw) for c, uw in zip(chains, uws)]
        for c, pre in zip(chains, pres):
            c["b"] = [pre[n * HEAD_DIM:(n + 1) * HEAD_DIM, :HEAD_DIM] for n in range(N_SUB)]
            qw = pre[N_SUB * HEAD_DIM:, HEAD_DIM:]
            c["o0"] = pre[N_SUB * HEAD_DIM:, :HEAD_DIM]
            c["lhs"] = [jnp.concatenate([-pre[n * HEAD_DIM:(n + 1) * HEAD_DIM, HEAD_DIM:],
                                         c["qe"][n * CHUNK:(n + 1) * CHUNK] - qw[n * CHUNK:(n + 1) * CHUNK]],
                                        axis=0).astype(BF16) for n in range(N_SUB)]
        for step in range(N_SUB):
            order = [step if c["d"] == 0 else N_SUB - 1 - step for c in chains]
            prods = [_dot(c["lhs"][n], c["state"].astype(BF16)) for c, n in zip(chains, order)]
            for c, n, prod in zip(chains, order, prods):
                c["outs"][n] = prod[HEAD_DIM:] + c["o0"][n * CHUNK:(n + 1) * CHUNK]
                c["state"] = c["state"] * jnp.exp(c["tot"][n]) + (prod[:HEAD_DIM] + c["b"][n])
        for c in chains:
            lanes = slice(c["hv"] * HEAD_DIM, (c["hv"] + 1) * HEAD_DIM)
            o_ref[pl.ds(c["r0"], SUPER), lanes] += jnp.concatenate(c["outs"], axis=0)
            s_ref[c["d"] * 2 * KH_STEP + c["hv"]] = c["state"]
        return carry

    lax.fori_loop(0, n_super, super_step, 0)

    gain = gain_ref[...]

    def norm_tile(i, carry):
        r0 = pl.multiple_of(i * rows, rows)
        o = o_ref[pl.ds(r0, rows), :]
        z = z_ref[0, pl.ds(r0, rows), :].astype(F32)
        parts = []
        for hv in range(2 * KH_STEP):
            oh = o[:, hv * HEAD_DIM:(hv + 1) * HEAD_DIM]
            parts.append(oh * _rms_scale(oh) * gain)
        y = jnp.concatenate(parts, axis=-1) * (z * jax.nn.sigmoid(z))
        y_ref[0, pl.ds(r0, rows), :] = y.astype(y_ref.dtype)
        return carry

    lax.fori_loop(0, t // rows, norm_tile, 0)


def _gate_rows(ga, n_kh):
    b, t, _ = ga.shape
    fwd = ga[..., 0:2 * n_kh].reshape(b, t, n_kh, 1, 2)
    bwd = ga[..., 64:64 + 2 * n_kh].reshape(b, t, n_kh, 1, 2)
    g4 = jnp.concatenate([fwd, bwd], axis=3).reshape(b, t // SUPER, SUPER, n_kh, 4)
    g4 = jnp.transpose(g4, (0, 3, 1, 4, 2))
    return jnp.concatenate([g4, jnp.zeros_like(g4)], axis=3)


def _delta(q, k, v, proj, z_col0, ga, out_gain, *, rows=256):
    b, t, kdim = q.shape
    n_kh = kdim // HEAD_DIM
    vdim = v.shape[-1]
    kw, vw = KH_STEP * HEAD_DIM, 2 * KH_STEP * HEAD_DIM
    assert vdim == 2 * kdim and t % SUPER == 0 and z_col0 % vw == 0 and 2 * n_kh == 32 and n_kh % KH_STEP == 0
    rows = min(rows, t)
    n_super = t // SUPER
    z_off = z_col0 // vw
    gr = _gate_rows(ga, n_kh)
    return pl.pallas_call(
        functools.partial(_delta_body, n_super=n_super, rows=rows),
        grid=(b, n_kh // KH_STEP),
        in_specs=[
            pl.BlockSpec((1, t, kw), lambda i, h: (i, 0, h)),
            pl.BlockSpec((1, t, kw), lambda i, h: (i, 0, h)),
            pl.BlockSpec((1, t, vw), lambda i, h: (i, 0, h)),
            pl.BlockSpec((1, t, vw), lambda i, h: (i, 0, h + z_off)),
            pl.BlockSpec((1, t, 128), lambda i, h: (i, 0, 0)),
            pl.BlockSpec((1, KH_STEP, n_super, 8, SUPER), lambda i, h: (i, h, 0, 0, 0)),
            pl.BlockSpec((1, HEAD_DIM), lambda i, h: (0, 0)),
        ],
        out_specs=pl.BlockSpec((1, t, vw), lambda i, h: (i, 0, h)),
        out_shape=jax.ShapeDtypeStruct((b, t, vdim), BF16),
        scratch_shapes=[pltpu.VMEM((t, vw), F32), pltpu.VMEM((4 * KH_STEP, HEAD_DIM, HEAD_DIM), F32)],
        compiler_params=_params("parallel", "arbitrary"),
        name="delta",
    )(q, k, v, proj, ga, gr, out_gain.reshape(1, HEAD_DIM).astype(F32))


def _gated_deltanet(x, norm_gain, w_in16, conv_w, a_log, dt_bias, out_norm, w_out16):
    b, s, d = x.shape
    n_gate = 128
    qkvz_dim = w_in16.shape[1] - n_gate
    value_dim = w_out16.shape[0]
    qkv_dim = qkvz_dim - value_dim
    key_dim = (qkv_dim - value_dim) // 2
    x2 = x.reshape(b * s, d)
    proj = _norm_mm(x2, norm_gain, w_in16[:, :qkvz_dim], out_dtype=BF16).reshape(b, s, qkvz_dim)
    gates = _norm_mm(x2, norm_gain, w_in16[:, qkvz_dim:], out_dtype=F32).reshape(b, s, n_gate)
    ga = _gates(gates, a_log, dt_bias)
    q = _conv(proj, conv_w[:, :key_dim], col0=0, width=key_dim, l2_scale=HEAD_DIM ** -0.5)
    k = _conv(proj, conv_w[:, key_dim:2 * key_dim], col0=key_dim, width=key_dim, l2_scale=1.0)
    v = _conv(proj, conv_w[:, 2 * key_dim:], col0=2 * key_dim, width=value_dim, l2_scale=None)
    y = _delta(q, k, v, proj, qkv_dim, ga, out_norm)
    return _mm_res(y.reshape(b * s, value_dim), w_out16, x2).reshape(b, s, d)


def _chan_dft_body(x_ref, xr_ref, xm_ref, g_ref, cs_ref, o_ref):
    first = pl.program_id(1) == 0
    g = g_ref[...]
    x = x_ref[0]
    xr = xr_ref[0]
    h = x * _rms_scale(x) * g
    hr = xr * _rms_scale(xr) * g
    is_pos0 = jnp.logical_and(first, lax.broadcasted_iota(jnp.int32, h.shape, 0) == 0)
    he = jnp.where(is_pos0, h, h + hr).astype(BF16)
    ho = (h - hr).astype(BF16)
    cs = cs_ref[...]
    for gi in range(x.shape[-1] // FN_GROUP_DIM):
        lo, hi = gi * FN_GROUP_DIM, (gi + 1) * FN_GROUP_DIM
        o_ref[0, 0, :, lo:hi] = _dot(he[:, lo:hi], cs[:, :FN_GROUP_DIM]).astype(o_ref.dtype)
        o_ref[0, 1, :, lo:hi] = _dot(ho[:, lo:hi], cs[:, FN_GROUP_DIM:]).astype(o_ref.dtype)

    @pl.when(first)
    def _():
        xm = xm_ref[0]
        hm = (xm * _rms_scale(xm) * g).astype(BF16)
        row = lax.broadcasted_iota(jnp.int32, (8, FN_GROUP_DIM), 0)
        for gi in range(x.shape[-1] // FN_GROUP_DIM):
            lo, hi = gi * FN_GROUP_DIM, (gi + 1) * FN_GROUP_DIM
            pm = _dot(hm[:, lo:hi], cs[:, :FN_GROUP_DIM]).astype(o_ref.dtype)
            o_ref[0, 1, 0:8, lo:hi] = jnp.where(row == 0, pm, o_ref[0, 1, 0:8, lo:hi])


def _chan_dft(x, gain, cs, *, tm=512):
    b, s, d = x.shape
    half = s // 2
    tm = min(tm, half)
    assert half % tm == 0 and d % FN_GROUP_DIM == 0 and half % 8 == 0
    xr = jnp.concatenate([x[:, :1], jnp.flip(x[:, half + 1:], axis=1)], axis=1)
    return pl.pallas_call(
        _chan_dft_body,
        grid=(b, half // tm),
        in_specs=[
            pl.BlockSpec((1, tm, d), lambda i, j: (i, j, 0)),
            pl.BlockSpec((1, tm, d), lambda i, j: (i, j, 0)),
            pl.BlockSpec((1, 8, d), lambda i, j: (i, half // 8, 0)),
            pl.BlockSpec((1, d), lambda i, j: (0, 0)),
            pl.BlockSpec((FN_GROUP_DIM, 2 * FN_GROUP_DIM), lambda i, j: (0, 0)),
        ],
        out_specs=pl.BlockSpec((1, 2, tm, d), lambda i, j: (i, 0, j, 0)),
        out_shape=jax.ShapeDtypeStruct((b, 2, half, d), BF16),
        compiler_params=_params("parallel", "arbitrary"),
        name="chan_dft",
    )(x, xr, x, gain.reshape(1, d), cs)


def _seq_dft_body(a_ref, b_ref, o_ref, acc_ref, *, n_k, scale):
    k = pl.program_id(3)
    p = _dot(a_ref[...], b_ref[0])

    @pl.when(k == 0)
    def _():
        acc_ref[...] = p

    @pl.when(k > 0)
    def _():
        acc_ref[...] += p

    @pl.when(k == n_k - 1)
    def _():
        o_ref[0] = (acc_ref[...] * scale).astype(o_ref.dtype)


def _seq_dft(dft, pq, *, scale, tm=2048, tn=1024, tk=1024):
    s, k2 = dft.shape
    b, _, d = pq.shape
    tm, tn, tk = min(tm, s), min(tn, d), min(tk, k2)
    assert s % tm == 0 and d % tn == 0 and k2 % tk == 0
    n_k = k2 // tk
    return pl.pallas_call(
        functools.partial(_seq_dft_body, n_k=n_k, scale=scale),
        grid=(b, s // tm, d // tn, n_k),
        in_specs=[
            pl.BlockSpec((tm, tk), lambda i, m, n, k: (m, k)),
            pl.BlockSpec((1, tk, tn), lambda i, m, n, k: (i, k, n)),
        ],
        out_specs=pl.BlockSpec((1, tm, tn), lambda i, m, n, k: (i, m, n)),
        out_shape=jax.ShapeDtypeStruct((b, s, d), BF16),
        scratch_shapes=[pltpu.VMEM((tm, tn), F32)],
        compiler_params=_params("parallel", "parallel", "parallel", "arbitrary"),
        name="seq_dft",
    )(dft, pq)


def _dft_tables(n_rows, n_cols, n):
    j = jnp.arange(n_rows, dtype=jnp.int32)
    k = jnp.arange(n_cols, dtype=jnp.int32)
    theta = ((j[:, None] * k[None, :]) % n).astype(F32) * (2.0 * math.pi / n)
    return jnp.cos(theta), jnp.sin(theta)


def _folded_seq_dft_matrix(s):
    half, fine = s // 2, 64
    sp = jnp.arange(s, dtype=jnp.int32)
    ka = fine * jnp.arange(half // fine, dtype=jnp.int32)
    kb = jnp.arange(fine, dtype=jnp.int32)
    ta = ((sp[:, None] * ka[None, :]) % s).astype(F32) * (2.0 * math.pi / s)
    tb = ((sp[:, None] * kb[None, :]) % s).astype(F32) * (2.0 * math.pi / s)
    ca, sa = jnp.cos(ta)[:, :, None], jnp.sin(ta)[:, :, None]
    cb, sb = jnp.cos(tb)[:, None, :], jnp.sin(tb)[:, None, :]
    cos = (ca * cb - sa * sb).reshape(s, half)
    sin = (sa * cb + ca * sb).reshape(s, half)
    sign = jnp.where(sp % 2 == 0, 1.0, -1.0).astype(F32)[:, None]
    neg_sin = jnp.where(jnp.arange(half)[None, :] == 0, sign, -sin)
    return jnp.concatenate([cos, neg_sin], axis=1).astype(BF16)


def _fourier_mixer(x, norm_gain, w_out16):
    b, s, d = x.shape
    cc, sc = _dft_tables(FN_GROUP_DIM, FN_GROUP_DIM, FN_GROUP_DIM)
    cs_c = jnp.concatenate([cc, sc], axis=1).astype(BF16)
    pq = _chan_dft(x, norm_gain, cs_c).reshape(b, s, d)
    mixed = _seq_dft(_folded_seq_dft_matrix(s), pq, scale=float((s * FN_GROUP_DIM) ** -0.5))
    return _mm_res(mixed.reshape(b * s, d), w_out16, x.reshape(b * s, d)).reshape(b, s, d)


def kernel(x_prompt, x_sample, ffn1_norm, ffn1_w_gate, ffn1_w_up, ffn1_w_down, mix_norm, dn_w_in, dn_conv_w, dn_a_log, dn_dt_bias, dn_out_norm, dn_w_out, fn_w_out, ffn2_norm, ffn2_w_gate, ffn2_w_up, ffn2_w_down, final_norm):
    depth = ffn1_norm.shape[0]
    n_mixers = 2
    f1 = [(ffn1_w_gate[i].astype(BF16), ffn1_w_up[i].astype(BF16), ffn1_w_down[i].astype(BF16)) for i in range(depth)]
    f2 = [(ffn2_w_gate[i].astype(BF16), ffn2_w_up[i].astype(BF16), ffn2_w_down[i].astype(BF16)) for i in range(depth)]
    dn_w_in16 = dn_w_in.astype(BF16)
    dn_w_out16 = dn_w_out.astype(BF16)
    fn_w_out16 = fn_w_out.astype(BF16)

    def ffn(x, gain, ws, final):
        b, s, d = x.shape
        return _ffn(x.reshape(b * s, d), gain, *ws, final_norm, final_norm=final).reshape(b, s, d)

    def trunk(x):
        for i in range(depth):
            x = ffn(x, ffn1_norm[i], f1[i], False)
            j = i // n_mixers
            if i % n_mixers == 0:
                x = _gated_deltanet(x, mix_norm[i], dn_w_in16[j], dn_conv_w[j], dn_a_log[j], dn_dt_bias[j],
                                    dn_out_norm[j], dn_w_out16[j])
            else:
                x = _fourier_mixer(x, mix_norm[i], fn_w_out16[j])
            x = ffn(x, ffn2_norm[i], f2[i], i == depth - 1)
        return x

    return (trunk(x_prompt), trunk(x_sample))
```

```python
import functools
import math

import jax
import jax.numpy as jnp
from jax import lax
from jax.experimental import pallas as pl
from jax.experimental.pallas import tpu as pltpu

EPS = 1e-6
CHUNK = 64
HEAD_DIM = 128
FN_GROUP_DIM = 256
CONV_WIDTH = 5
CONV_PAD = (CONV_WIDTH - 1) // 2
HALO = 8
VMEM_LIMIT_BYTES = 60 * 1024 * 1024
BF16 = jnp.bfloat16
F32 = jnp.float32
HIGHEST = lax.Precision.HIGHEST


def _params(*semantics):
    return pltpu.CompilerParams(dimension_semantics=semantics, vmem_limit_bytes=VMEM_LIMIT_BYTES)


def _rms_scale(x):
    return lax.rsqrt(jnp.mean(x * x, axis=-1, keepdims=True) + EPS)


def _dot(a, b):
    return jnp.dot(a, b, preferred_element_type=F32)


def _dot_nt(a, b):
    return lax.dot_general(a, b, (((1,), (1,)), ((), ())), preferred_element_type=F32)


def _ffn_body(x_ref, g_ref, wg_ref, wu_ref, wd_ref, fg_ref, o_ref, h_ref, *, n_f, final_norm):
    f = pl.program_id(1)

    @pl.when(f == 0)
    def _():
        x = x_ref[...]
        h_ref[...] = (x * _rms_scale(x) * g_ref[...]).astype(BF16)
        o_ref[...] = x

    h = h_ref[...]
    a = _dot(h, wg_ref[...])
    b = _dot(h, wu_ref[...])
    act = (a * jax.nn.sigmoid(a) * (b * 0.5)).astype(BF16)
    o_ref[...] += _dot(act, wd_ref[...])

    if final_norm:
        @pl.when(f == n_f - 1)
        def _():
            y = o_ref[...]
            o_ref[...] = y * _rms_scale(y) * fg_ref[...]


def _ffn(x, gain, wg, wu, wd, final_gain, *, final_norm, tm=1024, tf=512):
    m, d = x.shape
    f_dim = wg.shape[1]
    tm = min(tm, m)
    assert m % tm == 0 and f_dim % tf == 0
    n_f = f_dim // tf
    return pl.pallas_call(
        functools.partial(_ffn_body, n_f=n_f, final_norm=final_norm),
        grid=(m // tm, n_f),
        in_specs=[
            pl.BlockSpec((tm, d), lambda i, f: (i, 0)),
            pl.BlockSpec((1, d), lambda i, f: (0, 0)),
            pl.BlockSpec((d, tf), lambda i, f: (0, f)),
            pl.BlockSpec((d, tf), lambda i, f: (0, f)),
            pl.BlockSpec((tf, d), lambda i, f: (f, 0)),
            pl.BlockSpec((1, d), lambda i, f: (0, 0)),
        ],
        out_specs=pl.BlockSpec((tm, d), lambda i, f: (i, 0)),
        out_shape=jax.ShapeDtypeStruct((m, d), F32),
        scratch_shapes=[pltpu.VMEM((tm, d), BF16)],
        compiler_params=_params("parallel", "arbitrary"),
        name="ffn",
    )(x, gain.reshape(1, d), wg, wu, wd, final_gain.reshape(1, d))


def _norm_mm_body(x_ref, g_ref, w_ref, o_ref, h_ref):
    @pl.when(pl.program_id(1) == 0)
    def _():
        x = x_ref[...]
        h_ref[...] = (x * _rms_scale(x) * g_ref[...]).astype(BF16)

    o_ref[...] = _dot(h_ref[...], w_ref[...]).astype(o_ref.dtype)


def _norm_mm(x, gain, w, *, out_dtype, tm=1024, tn=1024):
    m, d = x.shape
    n = w.shape[1]
    tm, tn = min(tm, m), min(tn, n)
    assert m % tm == 0 and n % tn == 0
    return pl.pallas_call(
        _norm_mm_body,
        grid=(m // tm, n // tn),
        in_specs=[
            pl.BlockSpec((tm, d), lambda i, j: (i, 0)),
            pl.BlockSpec((1, d), lambda i, j: (0, 0)),
            pl.BlockSpec((d, tn), lambda i, j: (0, j)),
        ],
        out_specs=pl.BlockSpec((tm, tn), lambda i, j: (i, j)),
        out_shape=jax.ShapeDtypeStruct((m, n), out_dtype),
        scratch_shapes=[pltpu.VMEM((tm, d), BF16)],
        compiler_params=_params("parallel", "arbitrary"),
        name="norm_mm",
    )(x, gain.reshape(1, d), w)


def _mm_res_body(a_ref, w_ref, r_ref, o_ref):
    k = pl.program_id(2)
    p = _dot(a_ref[...], w_ref[...])

    @pl.when(k == 0)
    def _():
        o_ref[...] = r_ref[...] + p

    @pl.when(k > 0)
    def _():
        o_ref[...] += p


def _mm_res(a, w, res, *, tm=1024, tn=2048, tk=1024):
    m, kd = a.shape
    n = w.shape[1]
    tm, tn, tk = min(tm, m), min(tn, n), min(tk, kd)
    assert m % tm == 0 and n % tn == 0 and kd % tk == 0
    return pl.pallas_call(
        _mm_res_body,
        grid=(m // tm, n // tn, kd // tk),
        in_specs=[
            pl.BlockSpec((tm, tk), lambda i, j, k: (i, k)),
            pl.BlockSpec((tk, tn), lambda i, j, k: (k, j)),
            pl.BlockSpec((tm, tn), lambda i, j, k: (i, j)),
        ],
        out_specs=pl.BlockSpec((tm, tn), lambda i, j, k: (i, j)),
        out_shape=jax.ShapeDtypeStruct((m, n), F32),
        compiler_params=_params("parallel", "parallel", "arbitrary"),
        name="mm_res",
    )(a, w, res)


def _gates_body(x_ref, alog_ref, dtb_ref, ga_ref, *, n_chunks):
    lane = lax.broadcasted_iota(jnp.int32, (CHUNK, 128), 1)
    is_decay = (lane % 64) < 32
    is_bwd = lane >= 64
    row = lax.broadcasted_iota(jnp.int32, (CHUNK, CHUNK), 0)
    col = lax.broadcasted_iota(jnp.int32, (CHUNK, CHUNK), 1)
    lower = (row >= col).astype(F32)
    upper = (row <= col).astype(F32)
    neg_a = -jnp.exp(alog_ref[...])
    dtb = dtb_ref[...]
    for c in range(n_chunks):
        x = x_ref[0, c * CHUNK:(c + 1) * CHUNK, :]
        t = x + dtb
        softplus = jnp.maximum(t, 0.0) + jnp.log(1.0 + jnp.exp(-jnp.abs(t)))
        g = jnp.where(is_decay, neg_a * softplus, 0.0)
        beta = jax.nn.sigmoid(x)
        cum_f = jnp.dot(lower, g, precision=HIGHEST, preferred_element_type=F32)
        cum_b = jnp.dot(upper, g, precision=HIGHEST, preferred_element_type=F32)
        cum = jnp.where(is_bwd, cum_b, cum_f)
        ga_ref[0, c * CHUNK:(c + 1) * CHUNK, :] = jnp.where(is_decay, cum, beta)


def _gates(gates, a_log, dt_bias, *, tb=512):
    b, t, w = gates.shape
    tb = min(tb, t)
    assert t % tb == 0 and tb % CHUNK == 0 and w == 128
    zeros = jnp.zeros((32,), F32)
    alog_l = jnp.concatenate([a_log[0], zeros, a_log[1], zeros]).reshape(1, 128).astype(F32)
    dtb_l = jnp.concatenate([dt_bias[0], zeros, dt_bias[1], zeros]).reshape(1, 128).astype(F32)
    spec = pl.BlockSpec((1, tb, 128), lambda i, j: (i, j, 0))
    vec = pl.BlockSpec((1, 128), lambda i, j: (0, 0))
    return pl.pallas_call(
        functools.partial(_gates_body, n_chunks=tb // CHUNK),
        grid=(b, t // tb),
        in_specs=[spec, vec, vec],
        out_specs=spec,
        out_shape=jax.ShapeDtypeStruct((b, t, 128), F32),
        compiler_params=_params("parallel", "parallel"),
        name="gates",
    )(gates, alog_l, dtb_l)


def _conv_body(x_ref, w_ref, o_ref, pad_ref, *, t, rows, l2_scale):
    lanes = x_ref.shape[-1]
    zero = jnp.zeros((HALO, lanes), F32)
    pad_ref[0:HALO, :] = zero
    pad_ref[HALO + t:HALO + t + HALO, :] = zero
    pad_ref[HALO:HALO + t, :] = x_ref[0].astype(F32)
    w = w_ref[...]

    def tile(i, carry):
        r0 = pl.multiple_of(i * rows, rows)
        win = pad_ref[pl.ds(r0, rows + 2 * HALO), :]
        acc = None
        for j in range(CONV_WIDTH):
            s = HALO - CONV_PAD + j
            term = win[s:s + rows, :] * w[j:j + 1, :]
            acc = term if acc is None else acc + term
        y = acc * jax.nn.sigmoid(acc)
        if l2_scale is not None:
            parts = []
            for h in range(lanes // HEAD_DIM):
                yh = y[:, h * HEAD_DIM:(h + 1) * HEAD_DIM]
                inv = lax.rsqrt(jnp.sum(yh * yh, axis=-1, keepdims=True) + EPS)
                parts.append(yh * (inv * l2_scale))
            y = jnp.concatenate(parts, axis=-1)
        o_ref[0, pl.ds(r0, rows), :] = y.astype(o_ref.dtype)
        return carry

    lax.fori_loop(0, t // rows, tile, 0)


def _conv(proj, conv_w, *, col0, width, l2_scale, lanes=256, rows=256):
    b, t, _ = proj.shape
    rows = min(rows, t)
    assert width % lanes == 0 and col0 % lanes == 0 and t % rows == 0
    off = col0 // lanes
    w8 = jnp.zeros((8, width), F32).at[:CONV_WIDTH].set(conv_w.astype(F32))
    return pl.pallas_call(
        functools.partial(_conv_body, t=t, rows=rows, l2_scale=l2_scale),
        grid=(b, width // lanes),
        in_specs=[
            pl.BlockSpec((1, t, lanes), lambda i, j: (i, 0, j + off)),
            pl.BlockSpec((8, lanes), lambda i, j: (0, j)),
        ],
        out_specs=pl.BlockSpec((1, t, lanes), lambda i, j: (i, 0, j)),
        out_shape=jax.ShapeDtypeStruct((b, t, width), BF16),
        scratch_shapes=[pltpu.VMEM((t + 2 * HALO, lanes), F32)],
        compiler_params=_params("parallel", "parallel"),
        name="conv",
    )(proj, w8)


N_SUB = 4
SUPER = N_SUB * CHUNK
KH_STEP = 2


def _pack(mat):
    blk = lax.broadcasted_iota(jnp.int32, (CHUNK, SUPER), 1) // CHUNK
    out = mat[(N_SUB - 1) * CHUNK:, :]
    for n in range(N_SUB - 2, -1, -1):
        out = jnp.where(blk == n, mat[n * CHUNK:(n + 1) * CHUNK, :], out)
    return out


def _pack_col(colv):
    blk = lax.broadcasted_iota(jnp.int32, (CHUNK, SUPER), 1) // CHUNK
    out = jnp.broadcast_to(colv[(N_SUB - 1) * CHUNK:, :], (CHUNK, SUPER))
    for n in range(N_SUB - 2, -1, -1):
        out = jnp.where(blk == n, jnp.broadcast_to(colv[n * CHUNK:(n + 1) * CHUNK, :], (CHUNK, SUPER)), out)
    return out


def _block_diag(packed):
    r = lax.broadcasted_iota(jnp.int32, (SUPER, SUPER), 0) // CHUNK
    c = lax.broadcasted_iota(jnp.int32, (SUPER, SUPER), 1) // CHUNK
    return jnp.where(r == c, jnp.concatenate([packed] * N_SUB, axis=0), jnp.zeros((), packed.dtype))


def _unit_tri_inverse_packed(lps):
    i = lax.broadcasted_iota(jnp.int32, (CHUNK, SUPER), 0)
    j = lax.broadcasted_iota(jnp.int32, (CHUNK, SUPER), 1) % CHUNK
    eye = jnp.where(i == j, 1.0, 0.0)
    xs = [-lp for lp in lps]
    ps = [eye + x for x in xs]
    n_levels = int(math.log2(CHUNK)) - 1
    xs = [_dot(x.astype(BF16), _block_diag(x.astype(BF16))) for x in xs]
    for lvl in range(n_levels):
        rhss = [_block_diag(x.astype(BF16)) for x in xs]
        if lvl == n_levels - 1:
            ps = [p + _dot(p.astype(BF16), rhs) for p, rhs in zip(ps, rhss)]
        else:
            boths = [_dot(jnp.concatenate([x, p], axis=0).astype(BF16), rhs) for x, p, rhs in zip(xs, ps, rhss)]
            xs = [both[:CHUNK] for both in boths]
            ps = [p + both[CHUNK:] for p, both in zip(ps, boths)]
    return ps


def _delta_body(q_ref, k_ref, v_ref, z_ref, ga_ref, gr_ref, gain_ref, y_ref, o_ref, s_ref,
                *, n_super, rows):
    kh0 = pl.program_id(1) * KH_STEP
    t = n_super * SUPER
    o_ref[...] = jnp.zeros_like(o_ref)
    s_ref[...] = jnp.zeros_like(s_ref)

    pi = lax.broadcasted_iota(jnp.int32, (CHUNK, SUPER), 0)
    pj = lax.broadcasted_iota(jnp.int32, (CHUNK, SUPER), 1) % CHUNK
    gate_lane = lax.broadcasted_iota(jnp.int32, (SUPER, 128), 1)
    row_blk = lax.broadcasted_iota(jnp.int32, (SUPER, 1), 0) // CHUNK
    col_blk = lax.broadcasted_iota(jnp.int32, (HEAD_DIM, SUPER), 1) // CHUNK

    def super_step(s, carry):
        chains = []
        for d in range(2):
            ss = s if d == 0 else n_super - 1 - s
            r0 = pl.multiple_of(ss * SUPER, SUPER)
            incl = (pi >= pj) if d == 0 else (pi <= pj)
            strict = (pi > pj) if d == 0 else (pi < pj)
            ga = ga_ref[0, pl.ds(r0, SUPER), :]
            for kk in range(KH_STEP):
                q16 = q_ref[0, pl.ds(r0, SUPER), kk * HEAD_DIM:(kk + 1) * HEAD_DIM]
                k16 = k_ref[0, pl.ds(r0, SUPER), kk * HEAD_DIM:(kk + 1) * HEAD_DIM]
                grows = gr_ref[0, kk, ss]
                gram = _dot_nt(jnp.concatenate([k16, q16], axis=0), k16)
                kk_p = _pack(gram[:SUPER])
                qk_p = _pack(gram[SUPER:])
                q = q16.astype(F32)
                k = k16.astype(F32)
                for j in range(2):
                    hv = 2 * kk + j
                    lane_g = d * 64 + 2 * kh0 + hv
                    gcol = jnp.sum(jnp.where(gate_lane == lane_g, ga, 0.0), axis=-1, keepdims=True)
                    bcol = jnp.sum(jnp.where(gate_lane == lane_g + 32, ga, 0.0), axis=-1, keepdims=True)
                    grow = grows[d * 2 + j:d * 2 + j + 1, :]
                    diff = _pack_col(gcol) - grow
                    decay = jnp.where(incl, jnp.exp(jnp.where(incl, diff, 0.0)), 0.0)
                    eg = jnp.exp(gcol)
                    v = v_ref[0, pl.ds(r0, SUPER), hv * HEAD_DIM:(hv + 1) * HEAD_DIM].astype(F32)
                    ends = [n * CHUNK + (CHUNK - 1 if d == 0 else 0) for n in range(N_SUB)]
                    tot = [gcol[e:e + 1, :] for e in ends]
                    totcol = jnp.broadcast_to(tot[N_SUB - 1], (SUPER, 1))
                    for n in range(N_SUB - 2, -1, -1):
                        totcol = jnp.where(row_blk == n, tot[n], totcol)
                    kd_t = jnp.transpose(k * jnp.exp(totcol - gcol))
                    qkm_bd = _block_diag(jnp.where(incl, qk_p * decay, 0.0))
                    pre_lhs = jnp.concatenate(
                        [jnp.where(col_blk == n, kd_t, 0.0) for n in range(N_SUB)] + [qkm_bd], axis=0).astype(BF16)
                    chains.append(dict(
                        d=d, hv=hv, r0=r0, tot=tot, pre_lhs=pre_lhs,
                        lmat=jnp.where(strict, _pack_col(bcol) * kk_p * decay, 0.0),
                        rhs=jnp.concatenate([v * bcol, k * (bcol * eg)], axis=1).astype(BF16),
                        qe=q * eg,
                        state=s_ref[d * 2 * KH_STEP + hv],
                        outs=[None] * N_SUB))
        tmats = _unit_tri_inverse_packed([c["lmat"] for c in chains])
        uws = [_dot(_block_diag(tmat.astype(BF16)), c["rhs"]).astype(BF16) for c, tmat in zip(chains, tmats)]
        pres = [_dot(c["pre_lhs"], uw) for c, uw in zip(chains, uws)]
        for c, pre in zip(chains, pres):
            c["b"] = [pre[n * HEAD_DIM:(n + 1) * HEAD_DIM, :HEAD_DIM] for n in range(N_SUB)]
            qw = pre[N_SUB * HEAD_DIM:, HEAD_DIM:]
            c["o0"] = pre[N_SUB * HEAD_DIM:, :HEAD_DIM]
            c["lhs"] = [jnp.concatenate([-pre[n * HEAD_DIM:(n + 1) * HEAD_DIM, HEAD_DIM:],
                                         c["qe"][n * CHUNK:(n + 1) * CHUNK] - qw[n * CHUNK:(n + 1) * CHUNK]],
                                        axis=0).astype(BF16) for n in range(N_SUB)]
        for step in range(N_SUB):
            order = [step if c["d"] == 0 else N_SUB - 1 - step for c in chains]
            prods = [_dot(c["lhs"][n], c["state"].astype(BF16)) for c, n in zip(chains, order)]
            for c, n, prod in zip(chains, order, prods):
                c["outs"][n] = prod[HEAD_DIM:] + c["o0"][n * CHUNK:(n + 1) * CHUNK]
                c["state"] = c["state"] * jnp.exp(c["tot"][n]) + (prod[:HEAD_DIM] + c["b"][n])
        for c in chains:
            lanes = slice(c["hv"] * HEAD_DIM, (c["hv"] + 1) * HEAD_DIM)
            o_ref[pl.ds(c["r0"], SUPER), lanes] += jnp.concatenate(c["outs"], axis=0)
            s_ref[c["d"] * 2 * KH_STEP + c["hv"]] = c["state"]
        return carry

    lax.fori_loop(0, n_super, super_step, 0)

    gain = gain_ref[...]

    def norm_tile(i, carry):
        r0 = pl.multiple_of(i * rows, rows)
        o = o_ref[pl.ds(r0, rows), :]
        z = z_ref[0, pl.ds(r0, rows), :].astype(F32)
        parts = []
        for hv in range(2 * KH_STEP):
            oh = o[:, hv * HEAD_DIM:(hv + 1) * HEAD_DIM]
            parts.append(oh * _rms_scale(oh) * gain)
        y = jnp.concatenate(parts, axis=-1) * (z * jax.nn.sigmoid(z))
        y_ref[0, pl.ds(r0, rows), :] = y.astype(y_ref.dtype)
        return carry

    lax.fori_loop(0, t // rows, norm_tile, 0)


def _gate_rows(ga, n_kh):
    b, t, _ = ga.shape
    fwd = ga[..., 0:2 * n_kh].reshape(b, t, n_kh, 1, 2)
    bwd = ga[..., 64:64 + 2 * n_kh].reshape(b, t, n_kh, 1, 2)
    g4 = jnp.concatenate([fwd, bwd], axis=3).reshape(b, t // SUPER, SUPER, n_kh, 4)
    g4 = jnp.transpose(g4, (0, 3, 1, 4, 2))
    return jnp.concatenate([g4, jnp.zeros_like(g4)], axis=3)


def _delta(q, k, v, proj, z_col0, ga, out_gain, *, rows=256):
    b, t, kdim = q.shape
    n_kh = kdim // HEAD_DIM
    vdim = v.shape[-1]
    kw, vw = KH_STEP * HEAD_DIM, 2 * KH_STEP * HEAD_DIM
    assert vdim == 2 * kdim and t % SUPER == 0 and z_col0 % vw == 0 and 2 * n_kh == 32 and n_kh % KH_STEP == 0
    rows = min(rows, t)
    n_super = t // SUPER
    z_off = z_col0 // vw
    gr = _gate_rows(ga, n_kh)
    return pl.pallas_call(
        functools.partial(_delta_body, n_super=n_super, rows=rows),
        grid=(b, n_kh // KH_STEP),
        in_specs=[
            pl.BlockSpec((1, t, kw), lambda i, h: (i, 0, h)),
            pl.BlockSpec((1, t, kw), lambda i, h: (i, 0, h)),
            pl.BlockSpec((1, t, vw), lambda i, h: (i, 0, h)),
            pl.BlockSpec((1, t, vw), lambda i, h: (i, 0, h + z_off)),
            pl.BlockSpec((1, t, 128), lambda i, h: (i, 0, 0)),
            pl.BlockSpec((1, KH_STEP, n_super, 8, SUPER), lambda i, h: (i, h, 0, 0, 0)),
            pl.BlockSpec((1, HEAD_DIM), lambda i, h: (0, 0)),
        ],
        out_specs=pl.BlockSpec((1, t, vw), lambda i, h: (i, 0, h)),
        out_shape=jax.ShapeDtypeStruct((b, t, vdim), BF16),
        scratch_shapes=[pltpu.VMEM((t, vw), F32), pltpu.VMEM((4 * KH_STEP, HEAD_DIM, HEAD_DIM), F32)],
        compiler_params=_params("parallel", "arbitrary"),
        name="delta",
    )(q, k, v, proj, ga, gr, out_gain.reshape(1, HEAD_DIM).astype(F32))


def _gated_deltanet(x, norm_gain, w_in16, conv_w, a_log, dt_bias, out_norm, w_out16):
    b, s, d = x.shape
    n_gate = 128
    qkvz_dim = w_in16.shape[1] - n_gate
    value_dim = w_out16.shape[0]
    qkv_dim = qkvz_dim - value_dim
    key_dim = (qkv_dim - value_dim) // 2
    x2 = x.reshape(b * s, d)
    proj = _norm_mm(x2, norm_gain, w_in16[:, :qkvz_dim], out_dtype=BF16).reshape(b, s, qkvz_dim)
    gates = _norm_mm(x2, norm_gain, w_in16[:, qkvz_dim:], out_dtype=F32).reshape(b, s, n_gate)
    ga = _gates(gates, a_log, dt_bias)
    q = _conv(proj, conv_w[:, :key_dim], col0=0, width=key_dim, l2_scale=HEAD_DIM ** -0.5)
    k = _conv(proj, conv_w[:, key_dim:2 * key_dim], col0=key_dim, width=key_dim, l2_scale=1.0)
    v = _conv(proj, conv_w[:, 2 * key_dim:], col0=2 * key_dim, width=value_dim, l2_scale=None)
    y = _delta(q, k, v, proj, qkv_dim, ga, out_norm)
    return _mm_res(y.reshape(b * s, value_dim), w_out16, x2).reshape(b, s, d)


def _chan_dft_body(x_ref, xp_ref, xq_ref, xm_ref, g_ref, rev_ref, cs_ref, o_ref):
    first = pl.program_id(1) == 0
    g = g_ref[...]
    x = x_ref[0]
    xp = xp_ref[0]
    xq = xq_ref[0]
    h = x * _rms_scale(x) * g
    hp = (xp * _rms_scale(xp) * g).astype(BF16)
    hq = xq * _rms_scale(xq) * g
    row = lax.broadcasted_iota(jnp.int32, h.shape, 0)
    hr = jnp.where(row == 0, hq[0:1, :], _dot(rev_ref[...], hp))
    he = jnp.where(jnp.logical_and(first, row == 0), h, h + hr).astype(BF16)
    ho = (h - hr).astype(BF16)
    cs = cs_ref[...]
    for gi in range(x.shape[-1] // FN_GROUP_DIM):
        lo, hi = gi * FN_GROUP_DIM, (gi + 1) * FN_GROUP_DIM
        o_ref[0, 0, :, lo:hi] = _dot(he[:, lo:hi], cs[:, :FN_GROUP_DIM]).astype(o_ref.dtype)
        o_ref[0, 1, :, lo:hi] = _dot(ho[:, lo:hi], cs[:, FN_GROUP_DIM:]).astype(o_ref.dtype)

    @pl.when(first)
    def _():
        xm = xm_ref[0]
        hm = (xm * _rms_scale(xm) * g).astype(BF16)
        row8 = lax.broadcasted_iota(jnp.int32, (8, FN_GROUP_DIM), 0)
        for gi in range(x.shape[-1] // FN_GROUP_DIM):
            lo, hi = gi * FN_GROUP_DIM, (gi + 1) * FN_GROUP_DIM
            pm = _dot(hm[:, lo:hi], cs[:, :FN_GROUP_DIM]).astype(o_ref.dtype)
            o_ref[0, 1, 0:8, lo:hi] = jnp.where(row8 == 0, pm, o_ref[0, 1, 0:8, lo:hi])


def _chan_dft(x, gain, cs, *, tm=512):
    b, s, d = x.shape
    half = s // 2
    tm = min(tm, half)
    assert half % tm == 0 and d % FN_GROUP_DIM == 0 and tm % 8 == 0
    n_blk = s // tm
    ri = jnp.arange(tm, dtype=jnp.int32)
    rev = (ri[:, None] + ri[None, :] == tm).astype(BF16)
    row8 = lambda i, j: (i, ((s - j * tm) % s) // 8, 0)
    return pl.pallas_call(
        _chan_dft_body,
        grid=(b, half // tm),
        in_specs=[
            pl.BlockSpec((1, tm, d), lambda i, j: (i, j, 0)),
            pl.BlockSpec((1, tm, d), lambda i, j: (i, n_blk - 1 - j, 0)),
            pl.BlockSpec((1, 8, d), row8),
            pl.BlockSpec((1, 8, d), lambda i, j: (i, half // 8, 0)),
            pl.BlockSpec((1, d), lambda i, j: (0, 0)),
            pl.BlockSpec((tm, tm), lambda i, j: (0, 0)),
            pl.BlockSpec((FN_GROUP_DIM, 2 * FN_GROUP_DIM), lambda i, j: (0, 0)),
        ],
        out_specs=pl.BlockSpec((1, 2, tm, d), lambda i, j: (i, 0, j, 0)),
        out_shape=jax.ShapeDtypeStruct((b, 2, half, d), BF16),
        compiler_params=_params("parallel", "arbitrary"),
        name="chan_dft",
    )(x, x, x, x, gain.reshape(1, d), rev, cs)


def _seq_dft_body(a_ref, b_ref, o_ref, acc_ref, *, n_k, scale):
    k = pl.program_id(3)
    p = _dot(a_ref[...], b_ref[0])

    @pl.when(k == 0)
    def _():
        acc_ref[...] = p

    @pl.when(k > 0)
    def _():
        acc_ref[...] += p

    @pl.when(k == n_k - 1)
    def _():
        o_ref[0] = (acc_ref[...] * scale).astype(o_ref.dtype)


def _seq_dft(dft, pq, *, scale, tm=2048, tn=1024, tk=1024):
    s, k2 = dft.shape
    b, _, d = pq.shape
    tm, tn, tk = min(tm, s), min(tn, d), min(tk, k2)
    assert s % tm == 0 and d % tn == 0 and k2 % tk == 0
    n_k = k2 // tk
    return pl.pallas_call(
        functools.partial(_seq_dft_body, n_k=n_k, scale=scale),
        grid=(b, s // tm, d // tn, n_k),
        in_specs=[
            pl.BlockSpec((tm, tk), lambda i, m, n, k: (m, k)),
            pl.BlockSpec((1, tk, tn), lambda i, m, n, k: (i, k, n)),
        ],
        out_specs=pl.BlockSpec((1, tm, tn), lambda i, m, n, k: (i, m, n)),
        out_shape=jax.ShapeDtypeStruct((b, s, d), BF16),
        scratch_shapes=[pltpu.VMEM((tm, tn), F32)],
        compiler_params=_params("parallel", "parallel", "parallel", "arbitrary"),
        name="seq_dft",
    )(dft, pq)


def _dft_tables(n_rows, n_cols, n):
    j = jnp.arange(n_rows, dtype=jnp.int32)
    k = jnp.arange(n_cols, dtype=jnp.int32)
    theta = ((j[:, None] * k[None, :]) % n).astype(F32) * (2.0 * math.pi / n)
    return jnp.cos(theta), jnp.sin(theta)


def _folded_seq_dft_matrix(s):
    half, fine = s // 2, 64
    sp = jnp.arange(s, dtype=jnp.int32)
    ka = fine * jnp.arange(half // fine, dtype=jnp.int32)
    kb = jnp.arange(fine, dtype=jnp.int32)
    ta = ((sp[:, None] * ka[None, :]) % s).astype(F32) * (2.0 * math.pi / s)
    tb = ((sp[:, None] * kb[None, :]) % s).astype(F32) * (2.0 * math.pi / s)
    ca, sa = jnp.cos(ta)[:, :, None], jnp.sin(ta)[:, :, None]
    cb, sb = jnp.cos(tb)[:, None, :], jnp.sin(tb)[:, None, :]
    cos = (ca * cb - sa * sb).reshape(s, half)
    sin = (sa * cb + ca * sb).reshape(s, half)
    sign = jnp.where(sp % 2 == 0, 1.0, -1.0).astype(F32)[:, None]
    neg_sin = jnp.where(jnp.arange(half)[None, :] == 0, sign, -sin)
    return jnp.concatenate([cos, neg_sin], axis=1).astype(BF16)


def _fourier_mixer(x, norm_gain, w_out16):
    b, s, d = x.shape
    cc, sc = _dft_tables(FN_GROUP_DIM, FN_GROUP_DIM, FN_GROUP_DIM)
    cs_c = jnp.concatenate([cc, sc], axis=1).astype(BF16)
    pq = _chan_dft(x, norm_gain, cs_c).reshape(b, s, d)
    mixed = _seq_dft(_folded_seq_dft_matrix(s), pq, scale=float((s * FN_GROUP_DIM) ** -0.5))
    return _mm_res(mixed.reshape(b * s, d), w_out16, x.reshape(b * s, d)).reshape(b, s, d)


def kernel(x_prompt, x_sample, ffn1_norm, ffn1_w_gate, ffn1_w_up, ffn1_w_down, mix_norm, dn_w_in, dn_conv_w, dn_a_log, dn_dt_bias, dn_out_norm, dn_w_out, fn_w_out, ffn2_norm, ffn2_w_gate, ffn2_w_up, ffn2_w_down, final_norm):
    depth = ffn1_norm.shape[0]
    n_mixers = 2
    f1 = [(ffn1_w_gate[i].astype(BF16), ffn1_w_up[i].astype(BF16), ffn1_w_down[i].astype(BF16)) for i in range(depth)]
    f2 = [(ffn2_w_gate[i].astype(BF16), ffn2_w_up[i].astype(BF16), ffn2_w_down[i].astype(BF16)) for i in range(depth)]
    dn_w_in16 = dn_w_in.astype(BF16)
    dn_w_out16 = dn_w_out.astype(BF16)
    fn_w_out16 = fn_w_out.astype(BF16)

    def ffn(x, gain, ws, final):
        b, s, d = x.shape
        return _ffn(x.reshape(b * s, d), gain, *ws, final_norm, final_norm=final).reshape(b, s, d)

    def trunk(x):
        for i in range(depth):
            x = ffn(x, ffn1_norm[i], f1[i], False)
            j = i // n_mixers
            if i % n_mixers == 0:
                x = _gated_deltanet(x, mix_norm[i], dn_w_in16[j], dn_conv_w[j], dn_a_log[j], dn_dt_bias[j],
                                    dn_out_norm[j], dn_w_out16[j])
            else:
                x = _fourier_mixer(x, mix_norm[i], fn_w_out16[j])
            x = ffn(x, ffn2_norm[i], f2[i], i == depth - 1)
        return x

    return (trunk(x_prompt), trunk(x_sample))
```

```python
import functools
import math

import jax
import jax.numpy as jnp
from jax import lax
from jax.experimental import pallas as pl
from jax.experimental.pallas import tpu as pltpu

EPS = 1e-6
CHUNK = 64
HEAD_DIM = 128
FN_GROUP_DIM = 256
CONV_WIDTH = 5
CONV_PAD = (CONV_WIDTH - 1) // 2
HALO = 8
VMEM_LIMIT_BYTES = 60 * 1024 * 1024
BF16 = jnp.bfloat16
F32 = jnp.float32
HIGHEST = lax.Precision.HIGHEST


def _params(*semantics):
    return pltpu.CompilerParams(dimension_semantics=semantics, vmem_limit_bytes=VMEM_LIMIT_BYTES)


def _rms_scale(x):
    return lax.rsqrt(jnp.mean(x * x, axis=-1, keepdims=True) + EPS)


def _dot(a, b):
    return jnp.dot(a, b, preferred_element_type=F32)


def _dot_nt(a, b):
    return lax.dot_general(a, b, (((1,), (1,)), ((), ())), preferred_element_type=F32)


def _ffn_body(x_ref, g_ref, wg_ref, wu_ref, wd_ref, fg_ref, o_ref, h_ref, *, n_f, final_norm):
    f = pl.program_id(1)

    @pl.when(f == 0)
    def _():
        x = x_ref[...]
        h_ref[...] = (x * _rms_scale(x) * g_ref[...]).astype(BF16)
        o_ref[...] = x

    h = h_ref[...]
    a = _dot(h, wg_ref[...])
    b = _dot(h, wu_ref[...])
    act = (a * jax.nn.sigmoid(a) * (b * 0.5)).astype(BF16)
    o_ref[...] += _dot(act, wd_ref[...])

    if final_norm:
        @pl.when(f == n_f - 1)
        def _():
            y = o_ref[...]
            o_ref[...] = y * _rms_scale(y) * fg_ref[...]


def _ffn(x, gain, wg, wu, wd, final_gain, *, final_norm, tm=1024, tf=512):
    m, d = x.shape
    f_dim = wg.shape[1]
    tm = min(tm, m)
    assert m % tm == 0 and f_dim % tf == 0
    n_f = f_dim // tf
    return pl.pallas_call(
        functools.partial(_ffn_body, n_f=n_f, final_norm=final_norm),
        grid=(m // tm, n_f),
        in_specs=[
            pl.BlockSpec((tm, d), lambda i, f: (i, 0)),
            pl.BlockSpec((1, d), lambda i, f: (0, 0)),
            pl.BlockSpec((d, tf), lambda i, f: (0, f)),
            pl.BlockSpec((d, tf), lambda i, f: (0, f)),
            pl.BlockSpec((tf, d), lambda i, f: (f, 0)),
            pl.BlockSpec((1, d), lambda i, f: (0, 0)),
        ],
        out_specs=pl.BlockSpec((tm, d), lambda i, f: (i, 0)),
        out_shape=jax.ShapeDtypeStruct((m, d), F32),
        scratch_shapes=[pltpu.VMEM((tm, d), BF16)],
        compiler_params=_params("parallel", "arbitrary"),
        name="ffn",
    )(x, gain.reshape(1, d), wg, wu, wd, final_gain.reshape(1, d))


def _norm_mm_body(x_ref, g_ref, w_ref, wn_ref, o_ref, on_ref, h_ref):
    @pl.when(pl.program_id(1) == 0)
    def _():
        x = x_ref[...]
        h = (x * _rms_scale(x) * g_ref[...]).astype(BF16)
        h_ref[...] = h
        on_ref[...] = _dot(h, wn_ref[...])

    o_ref[...] = _dot(h_ref[...], w_ref[...]).astype(o_ref.dtype)


def _norm_mm(x, gain, w, w_narrow, *, tm=1024, tn=1024):
    m, d = x.shape
    n, nn = w.shape[1], w_narrow.shape[1]
    tm, tn = min(tm, m), min(tn, n)
    assert m % tm == 0 and n % tn == 0
    return pl.pallas_call(
        _norm_mm_body,
        grid=(m // tm, n // tn),
        in_specs=[
            pl.BlockSpec((tm, d), lambda i, j: (i, 0)),
            pl.BlockSpec((1, d), lambda i, j: (0, 0)),
            pl.BlockSpec((d, tn), lambda i, j: (0, j)),
            pl.BlockSpec((d, nn), lambda i, j: (0, 0)),
        ],
        out_specs=[pl.BlockSpec((tm, tn), lambda i, j: (i, j)), pl.BlockSpec((tm, nn), lambda i, j: (i, 0))],
        out_shape=[jax.ShapeDtypeStruct((m, n), BF16), jax.ShapeDtypeStruct((m, nn), F32)],
        scratch_shapes=[pltpu.VMEM((tm, d), BF16)],
        compiler_params=_params("parallel", "arbitrary"),
        name="norm_mm",
    )(x, gain.reshape(1, d), w, w_narrow)


def _mm_res_body(a_ref, w_ref, r_ref, o_ref):
    k = pl.program_id(2)
    p = _dot(a_ref[...], w_ref[...])

    @pl.when(k == 0)
    def _():
        o_ref[...] = r_ref[...] + p

    @pl.when(k > 0)
    def _():
        o_ref[...] += p


def _mm_res(a, w, res, *, tm=1024, tn=2048, tk=1024):
    m, kd = a.shape
    n = w.shape[1]
    tm, tn, tk = min(tm, m), min(tn, n), min(tk, kd)
    assert m % tm == 0 and n % tn == 0 and kd % tk == 0
    return pl.pallas_call(
        _mm_res_body,
        grid=(m // tm, n // tn, kd // tk),
        in_specs=[
            pl.BlockSpec((tm, tk), lambda i, j, k: (i, k)),
            pl.BlockSpec((tk, tn), lambda i, j, k: (k, j)),
            pl.BlockSpec((tm, tn), lambda i, j, k: (i, j)),
        ],
        out_specs=pl.BlockSpec((tm, tn), lambda i, j, k: (i, j)),
        out_shape=jax.ShapeDtypeStruct((m, n), F32),
        compiler_params=_params("parallel", "parallel", "arbitrary"),
        name="mm_res",
    )(a, w, res)


def _gates_body(x_ref, alog_ref, dtb_ref, ga_ref, *, n_chunks):
    lane = lax.broadcasted_iota(jnp.int32, (CHUNK, 128), 1)
    is_decay = (lane % 64) < 32
    is_bwd = lane >= 64
    row = lax.broadcasted_iota(jnp.int32, (CHUNK, CHUNK), 0)
    col = lax.broadcasted_iota(jnp.int32, (CHUNK, CHUNK), 1)
    lower = (row >= col).astype(F32)
    upper = (row <= col).astype(F32)
    neg_a = -jnp.exp(alog_ref[...])
    dtb = dtb_ref[...]
    for c in range(n_chunks):
        x = x_ref[0, c * CHUNK:(c + 1) * CHUNK, :]
        t = x + dtb
        softplus = jnp.maximum(t, 0.0) + jnp.log(1.0 + jnp.exp(-jnp.abs(t)))
        g = jnp.where(is_decay, neg_a * softplus, 0.0)
        beta = jax.nn.sigmoid(x)
        cum_f = jnp.dot(lower, g, precision=HIGHEST, preferred_element_type=F32)
        cum_b = jnp.dot(upper, g, precision=HIGHEST, preferred_element_type=F32)
        cum = jnp.where(is_bwd, cum_b, cum_f)
        ga_ref[0, c * CHUNK:(c + 1) * CHUNK, :] = jnp.where(is_decay, cum, beta)


def _gates(gates, a_log, dt_bias, *, tb=512):
    b, t, w = gates.shape
    tb = min(tb, t)
    assert t % tb == 0 and tb % CHUNK == 0 and w == 128
    zeros = jnp.zeros((32,), F32)
    alog_l = jnp.concatenate([a_log[0], zeros, a_log[1], zeros]).reshape(1, 128).astype(F32)
    dtb_l = jnp.concatenate([dt_bias[0], zeros, dt_bias[1], zeros]).reshape(1, 128).astype(F32)
    spec = pl.BlockSpec((1, tb, 128), lambda i, j: (i, j, 0))
    vec = pl.BlockSpec((1, 128), lambda i, j: (0, 0))
    return pl.pallas_call(
        functools.partial(_gates_body, n_chunks=tb // CHUNK),
        grid=(b, t // tb),
        in_specs=[spec, vec, vec],
        out_specs=spec,
        out_shape=jax.ShapeDtypeStruct((b, t, 128), F32),
        compiler_params=_params("parallel", "parallel"),
        name="gates",
    )(gates, alog_l, dtb_l)


def _conv_body(x_ref, w_ref, o_ref, pad_ref, *, t, rows, l2_scale):
    lanes = x_ref.shape[-1]
    zero = jnp.zeros((HALO, lanes), F32)
    pad_ref[0:HALO, :] = zero
    pad_ref[HALO + t:HALO + t + HALO, :] = zero
    pad_ref[HALO:HALO + t, :] = x_ref[0].astype(F32)
    w = w_ref[...]

    def tile(i, carry):
        r0 = pl.multiple_of(i * rows, rows)
        win = pad_ref[pl.ds(r0, rows + 2 * HALO), :]
        acc = None
        for j in range(CONV_WIDTH):
            s = HALO - CONV_PAD + j
            term = win[s:s + rows, :] * w[j:j + 1, :]
            acc = term if acc is None else acc + term
        y = acc * jax.nn.sigmoid(acc)
        if l2_scale is not None:
            parts = []
            for h in range(lanes // HEAD_DIM):
                yh = y[:, h * HEAD_DIM:(h + 1) * HEAD_DIM]
                inv = lax.rsqrt(jnp.sum(yh * yh, axis=-1, keepdims=True) + EPS)
                parts.append(yh * (inv * l2_scale))
            y = jnp.concatenate(parts, axis=-1)
        o_ref[0, pl.ds(r0, rows), :] = y.astype(o_ref.dtype)
        return carry

    lax.fori_loop(0, t // rows, tile, 0)


def _conv(proj, conv_w, *, col0, width, l2_scale, lanes=256, rows=256):
    b, t, _ = proj.shape
    rows = min(rows, t)
    assert width % lanes == 0 and col0 % lanes == 0 and t % rows == 0
    off = col0 // lanes
    w8 = jnp.zeros((8, width), F32).at[:CONV_WIDTH].set(conv_w.astype(F32))
    return pl.pallas_call(
        functools.partial(_conv_body, t=t, rows=rows, l2_scale=l2_scale),
        grid=(b, width // lanes),
        in_specs=[
            pl.BlockSpec((1, t, lanes), lambda i, j: (i, 0, j + off)),
            pl.BlockSpec((8, lanes), lambda i, j: (0, j)),
        ],
        out_specs=pl.BlockSpec((1, t, lanes), lambda i, j: (i, 0, j)),
        out_shape=jax.ShapeDtypeStruct((b, t, width), BF16),
        scratch_shapes=[pltpu.VMEM((t + 2 * HALO, lanes), F32)],
        compiler_params=_params("parallel", "parallel"),
        name="conv",
    )(proj, w8)


N_SUB = 4
SUPER = N_SUB * CHUNK
KH_STEP = 2


def _pack(mat):
    blk = lax.broadcasted_iota(jnp.int32, (CHUNK, SUPER), 1) // CHUNK
    out = mat[(N_SUB - 1) * CHUNK:, :]
    for n in range(N_SUB - 2, -1, -1):
        out = jnp.where(blk == n, mat[n * CHUNK:(n + 1) * CHUNK, :], out)
    return out


def _pack_col(colv):
    blk = lax.broadcasted_iota(jnp.int32, (CHUNK, SUPER), 1) // CHUNK
    out = jnp.broadcast_to(colv[(N_SUB - 1) * CHUNK:, :], (CHUNK, SUPER))
    for n in range(N_SUB - 2, -1, -1):
        out = jnp.where(blk == n, jnp.broadcast_to(colv[n * CHUNK:(n + 1) * CHUNK, :], (CHUNK, SUPER)), out)
    return out


def _block_diag(packed):
    r = lax.broadcasted_iota(jnp.int32, (SUPER, SUPER), 0) // CHUNK
    c = lax.broadcasted_iota(jnp.int32, (SUPER, SUPER), 1) // CHUNK
    return jnp.where(r == c, jnp.concatenate([packed] * N_SUB, axis=0), jnp.zeros((), packed.dtype))


def _unit_tri_inverse_packed(lps):
    i = lax.broadcasted_iota(jnp.int32, (CHUNK, SUPER), 0)
    j = lax.broadcasted_iota(jnp.int32, (CHUNK, SUPER), 1) % CHUNK
    eye = jnp.where(i == j, 1.0, 0.0)
    xs = [-lp for lp in lps]
    ps = [eye + x for x in xs]
    n_levels = int(math.log2(CHUNK)) - 1
    xs = [_dot(x.astype(BF16), _block_diag(x.astype(BF16))) for x in xs]
    for lvl in range(n_levels):
        rhss = [_block_diag(x.astype(BF16)) for x in xs]
        if lvl == n_levels - 1:
            ps = [p + _dot(p.astype(BF16), rhs) for p, rhs in zip(ps, rhss)]
        else:
            boths = [_dot(jnp.concatenate([x, p], axis=0).astype(BF16), rhs) for x, p, rhs in zip(xs, ps, rhss)]
            xs = [both[:CHUNK] for both in boths]
            ps = [p + both[CHUNK:] for p, both in zip(ps, boths)]
    return ps


def _delta_body(q_ref, k_ref, v_ref, z_ref, ga_ref, gr_ref, gain_ref, y_ref, o_ref, s_ref,
                *, n_super, rows):
    kh0 = pl.program_id(1) * KH_STEP
    t = n_super * SUPER
    o_ref[...] = jnp.zeros_like(o_ref)
    s_ref[...] = jnp.zeros_like(s_ref)

    pi = lax.broadcasted_iota(jnp.int32, (CHUNK, SUPER), 0)
    pj = lax.broadcasted_iota(jnp.int32, (CHUNK, SUPER), 1) % CHUNK
    gate_lane = lax.broadcasted_iota(jnp.int32, (SUPER, 128), 1)
    row_blk = lax.broadcasted_iota(jnp.int32, (SUPER, 1), 0) // CHUNK
    col_blk = lax.broadcasted_iota(jnp.int32, (HEAD_DIM, SUPER), 1) // CHUNK

    def super_step(s, carry):
        chains = []
        for d in range(2):
            ss = s if d == 0 else n_super - 1 - s
            r0 = pl.multiple_of(ss * SUPER, SUPER)
            incl = (pi >= pj) if d == 0 else (pi <= pj)
            strict = (pi > pj) if d == 0 else (pi < pj)
            ga = ga_ref[0, pl.ds(r0, SUPER), :]
            for kk in range(KH_STEP):
                q16 = q_ref[0, pl.ds(r0, SUPER), kk * HEAD_DIM:(kk + 1) * HEAD_DIM]
                k16 = k_ref[0, pl.ds(r0, SUPER), kk * HEAD_DIM:(kk + 1) * HEAD_DIM]
                grows = gr_ref[0, kk, ss]
                gram = _dot_nt(jnp.concatenate([k16, q16], axis=0), k16)
                kk_p = _pack(gram[:SUPER])
                qk_p = _pack(gram[SUPER:])
                q = q16.astype(F32)
                k = k16.astype(F32)
                for j in range(2):
                    hv = 2 * kk + j
                    lane_g = d * 64 + 2 * kh0 + hv
                    gcol = jnp.sum(jnp.where(gate_lane == lane_g, ga, 0.0), axis=-1, keepdims=True)
                    bcol = jnp.sum(jnp.where(gate_lane == lane_g + 32, ga, 0.0), axis=-1, keepdims=True)
                    grow = grows[d * 2 + j:d * 2 + j + 1, :]
                    diff = _pack_col(gcol) - grow
                    decay = jnp.where(incl, jnp.exp(jnp.where(incl, diff, 0.0)), 0.0)
                    eg = jnp.exp(gcol)
                    v = v_ref[0, pl.ds(r0, SUPER), hv * HEAD_DIM:(hv + 1) * HEAD_DIM].astype(F32)
                    ends = [n * CHUNK + (CHUNK - 1 if d == 0 else 0) for n in range(N_SUB)]
                    tot = [gcol[e:e + 1, :] for e in ends]
                    totcol = jnp.broadcast_to(tot[N_SUB - 1], (SUPER, 1))
                    for n in range(N_SUB - 2, -1, -1):
                        totcol = jnp.where(row_blk == n, tot[n], totcol)
                    kd_t = jnp.transpose(k * jnp.exp(totcol - gcol))
                    qkm_bd = _block_diag(jnp.where(incl, qk_p * decay, 0.0))
                    pre_lhs = jnp.concatenate(
                        [jnp.where(col_blk == n, kd_t, 0.0) for n in range(N_SUB)] + [qkm_bd], axis=0).astype(BF16)
                    chains.append(dict(
                        d=d, hv=hv, r0=r0, tot=tot, pre_lhs=pre_lhs,
                        lmat=jnp.where(strict, _pack_col(bcol) * kk_p * decay, 0.0),
                        rhs=jnp.concatenate([v * bcol, k * (bcol * eg)], axis=1).astype(BF16),
                        qe=q * eg,
                        state=s_ref[d * 2 * KH_STEP + hv],
                        outs=[None] * N_SUB))
        tmats = _unit_tri_inverse_packed([c["lmat"] for c in chains])
        uws = [_dot(_block_diag(tmat.astype(BF16)), c["rhs"]).astype(BF16) for c, tmat in zip(chains, tmats)]
        pres = [_dot(c["pre_lhs"], uw) for c, uw in zip(chains, uws)]
        for c, pre in zip(chains, pres):
            c["b"] = [pre[n * HEAD_DIM:(n + 1) * HEAD_DIM, :HEAD_DIM] for n in range(N_SUB)]
            qw = pre[N_SUB * HEAD_DIM:, HEAD_DIM:]
            c["o0"] = pre[N_SUB * HEAD_DIM:, :HEAD_DIM]
            c["lhs"] = [jnp.concatenate([-pre[n * HEAD_DIM:(n + 1) * HEAD_DIM, HEAD_DIM:],
                                         c["qe"][n * CHUNK:(n + 1) * CHUNK] - qw[n * CHUNK:(n + 1) * CHUNK]],
                                        axis=0).astype(BF16) for n in range(N_SUB)]
        for step in range(N_SUB):
            order = [step if c["d"] == 0 else N_SUB - 1 - step for c in chains]
            prods = [_dot(c["lhs"][n], c["state"].astype(BF16)) for c, n in zip(chains, order)]
            for c, n, prod in zip(chains, order, prods):
                c["outs"][n] = prod[HEAD_DIM:] + c["o0"][n * CHUNK:(n + 1) * CHUNK]
                c["state"] = c["state"] * jnp.exp(c["tot"][n]) + (prod[:HEAD_DIM] + c["b"][n])
        for c in chains:
            lanes = slice(c["hv"] * HEAD_DIM, (c["hv"] + 1) * HEAD_DIM)
            o_ref[pl.ds(c["r0"], SUPER), lanes] += jnp.concatenate(c["outs"], axis=0)
            s_ref[c["d"] * 2 * KH_STEP + c["hv"]] = c["state"]
        return carry

    lax.fori_loop(0, n_super, super_step, 0)

    gain = gain_ref[...]

    def norm_tile(i, carry):
        r0 = pl.multiple_of(i * rows, rows)
        o = o_ref[pl.ds(r0, rows), :]
        z = z_ref[0, pl.ds(r0, rows), :].astype(F32)
        parts = []
        for hv in range(2 * KH_STEP):
            oh = o[:, hv * HEAD_DIM:(hv + 1) * HEAD_DIM]
            parts.append(oh * _rms_scale(oh) * gain)
        y = jnp.concatenate(parts, axis=-1) * (z * jax.nn.sigmoid(z))
        y_ref[0, pl.ds(r0, rows), :] = y.astype(y_ref.dtype)
        return carry

    lax.fori_loop(0, t // rows, norm_tile, 0)


def _gate_rows(ga, n_kh):
    b, t, _ = ga.shape
    fwd = ga[..., 0:2 * n_kh].reshape(b, t, n_kh, 1, 2)
    bwd = ga[..., 64:64 + 2 * n_kh].reshape(b, t, n_kh, 1, 2)
    g4 = jnp.concatenate([fwd, bwd], axis=3).reshape(b, t // SUPER, SUPER, n_kh, 4)
    g4 = jnp.transpose(g4, (0, 3, 1, 4, 2))
    return jnp.concatenate([g4, jnp.zeros_like(g4)], axis=3)


def _delta(q, k, v, proj, z_col0, ga, out_gain, *, rows=256):
    b, t, kdim = q.shape
    n_kh = kdim // HEAD_DIM
    vdim = v.shape[-1]
    kw, vw = KH_STEP * HEAD_DIM, 2 * KH_STEP * HEAD_DIM
    assert vdim == 2 * kdim and t % SUPER == 0 and z_col0 % vw == 0 and 2 * n_kh == 32 and n_kh % KH_STEP == 0
    rows = min(rows, t)
    n_super = t // SUPER
    z_off = z_col0 // vw
    gr = _gate_rows(ga, n_kh)
    return pl.pallas_call(
        functools.partial(_delta_body, n_super=n_super, rows=rows),
        grid=(b, n_kh // KH_STEP),
        in_specs=[
            pl.BlockSpec((1, t, kw), lambda i, h: (i, 0, h)),
            pl.BlockSpec((1, t, kw), lambda i, h: (i, 0, h)),
            pl.BlockSpec((1, t, vw), lambda i, h: (i, 0, h)),
            pl.BlockSpec((1, t, vw), lambda i, h: (i, 0, h + z_off)),
            pl.BlockSpec((1, t, 128), lambda i, h: (i, 0, 0)),
            pl.BlockSpec((1, KH_STEP, n_super, 8, SUPER), lambda i, h: (i, h, 0, 0, 0)),
            pl.BlockSpec((1, HEAD_DIM), lambda i, h: (0, 0)),
        ],
        out_specs=pl.BlockSpec((1, t, vw), lambda i, h: (i, 0, h)),
        out_shape=jax.ShapeDtypeStruct((b, t, vdim), BF16),
        scratch_shapes=[pltpu.VMEM((t, vw), F32), pltpu.VMEM((4 * KH_STEP, HEAD_DIM, HEAD_DIM), F32)],
        compiler_params=_params("parallel", "arbitrary"),
        name="delta",
    )(q, k, v, proj, ga, gr, out_gain.reshape(1, HEAD_DIM).astype(F32))


def _gated_deltanet(x, norm_gain, w_in16, conv_w, a_log, dt_bias, out_norm, w_out16):
    b, s, d = x.shape
    n_gate = 128
    qkvz_dim = w_in16.shape[1] - n_gate
    value_dim = w_out16.shape[0]
    qkv_dim = qkvz_dim - value_dim
    key_dim = (qkv_dim - value_dim) // 2
    x2 = x.reshape(b * s, d)
    proj, gates = _norm_mm(x2, norm_gain, w_in16[:, :qkvz_dim], w_in16[:, qkvz_dim:])
    proj, gates = proj.reshape(b, s, qkvz_dim), gates.reshape(b, s, n_gate)
    ga = _gates(gates, a_log, dt_bias)
    q = _conv(proj, conv_w[:, :key_dim], col0=0, width=key_dim, l2_scale=HEAD_DIM ** -0.5)
    k = _conv(proj, conv_w[:, key_dim:2 * key_dim], col0=key_dim, width=key_dim, l2_scale=1.0)
    v = _conv(proj, conv_w[:, 2 * key_dim:], col0=2 * key_dim, width=value_dim, l2_scale=None)
    y = _delta(q, k, v, proj, qkv_dim, ga, out_norm)
    return _mm_res(y.reshape(b * s, value_dim), w_out16, x2).reshape(b, s, d)


def _chan_dft_body(x_ref, xp_ref, xq_ref, xm_ref, g_ref, rev_ref, cs_ref, o_ref):
    first = pl.program_id(1) == 0
    g = g_ref[...]
    x = x_ref[0]
    xp = xp_ref[0]
    xq = xq_ref[0]
    h = x * _rms_scale(x) * g
    hp = (xp * _rms_scale(xp) * g).astype(BF16)
    hq = xq * _rms_scale(xq) * g
    row = lax.broadcasted_iota(jnp.int32, h.shape, 0)
    hr = jnp.where(row == 0, hq[0:1, :], _dot(rev_ref[...], hp))
    he = jnp.where(jnp.logical_and(first, row == 0), h, h + hr).astype(BF16)
    ho = (h - hr).astype(BF16)
    cs = cs_ref[...]
    for gi in range(x.shape[-1] // FN_GROUP_DIM):
        lo, hi = gi * FN_GROUP_DIM, (gi + 1) * FN_GROUP_DIM
        o_ref[0, 0, :, lo:hi] = _dot(he[:, lo:hi], cs[:, :FN_GROUP_DIM]).astype(o_ref.dtype)
        o_ref[0, 1, :, lo:hi] = _dot(ho[:, lo:hi], cs[:, FN_GROUP_DIM:]).astype(o_ref.dtype)

    @pl.when(first)
    def _():
        xm = xm_ref[0]
        hm = (xm * _rms_scale(xm) * g).astype(BF16)
        row8 = lax.broadcasted_iota(jnp.int32, (8, FN_GROUP_DIM), 0)
        for gi in range(x.shape[-1] // FN_GROUP_DIM):
            lo, hi = gi * FN_GROUP_DIM, (gi + 1) * FN_GROUP_DIM
            pm = _dot(hm[:, lo:hi], cs[:, :FN_GROUP_DIM]).astype(o_ref.dtype)
            o_ref[0, 1, 0:8, lo:hi] = jnp.where(row8 == 0, pm, o_ref[0, 1, 0:8, lo:hi])


def _chan_dft(x, gain, cs, *, tm=512):
    b, s, d = x.shape
    half = s // 2
    tm = min(tm, half)
    assert half % tm == 0 and d % FN_GROUP_DIM == 0 and tm % 8 == 0
    n_blk = s // tm
    ri = jnp.arange(tm, dtype=jnp.int32)
    rev = (ri[:, None] + ri[None, :] == tm).astype(BF16)
    row8 = lambda i, j: (i, ((s - j * tm) % s) // 8, 0)
    return pl.pallas_call(
        _chan_dft_body,
        grid=(b, half // tm),
        in_specs=[
            pl.BlockSpec((1, tm, d), lambda i, j: (i, j, 0)),
            pl.BlockSpec((1, tm, d), lambda i, j: (i, n_blk - 1 - j, 0)),
            pl.BlockSpec((1, 8, d), row8),
            pl.BlockSpec((1, 8, d), lambda i, j: (i, half // 8, 0)),
            pl.BlockSpec((1, d), lambda i, j: (0, 0)),
            pl.BlockSpec((tm, tm), lambda i, j: (0, 0)),
            pl.BlockSpec((FN_GROUP_DIM, 2 * FN_GROUP_DIM), lambda i, j: (0, 0)),
        ],
        out_specs=pl.BlockSpec((1, 2, tm, d), lambda i, j: (i, 0, j, 0)),
        out_shape=jax.ShapeDtypeStruct((b, 2, half, d), BF16),
        compiler_params=_params("parallel", "arbitrary"),
        name="chan_dft",
    )(x, x, x, x, gain.reshape(1, d), rev, cs)


def _seq_dft_body(a_ref, b_ref, o_ref, acc_ref, *, n_k, scale):
    k = pl.program_id(3)
    p = _dot(a_ref[...], b_ref[0])

    @pl.when(k == 0)
    def _():
        acc_ref[...] = p

    @pl.when(k > 0)
    def _():
        acc_ref[...] += p

    @pl.when(k == n_k - 1)
    def _():
        o_ref[0] = (acc_ref[...] * scale).astype(o_ref.dtype)


def _seq_dft(dft, pq, *, scale, tm=2048, tn=1024, tk=1024):
    s, k2 = dft.shape
    b, _, d = pq.shape
    tm, tn, tk = min(tm, s), min(tn, d), min(tk, k2)
    assert s % tm == 0 and d % tn == 0 and k2 % tk == 0
    n_k = k2 // tk
    return pl.pallas_call(
        functools.partial(_seq_dft_body, n_k=n_k, scale=scale),
        grid=(b, s // tm, d // tn, n_k),
        in_specs=[
            pl.BlockSpec((tm, tk), lambda i, m, n, k: (m, k)),
            pl.BlockSpec((1, tk, tn), lambda i, m, n, k: (i, k, n)),
        ],
        out_specs=pl.BlockSpec((1, tm, tn), lambda i, m, n, k: (i, m, n)),
        out_shape=jax.ShapeDtypeStruct((b, s, d), BF16),
        scratch_shapes=[pltpu.VMEM((tm, tn), F32)],
        compiler_params=_params("parallel", "parallel", "parallel", "arbitrary"),
        name="seq_dft",
    )(dft, pq)


def _dft_tables(n_rows, n_cols, n):
    j = jnp.arange(n_rows, dtype=jnp.int32)
    k = jnp.arange(n_cols, dtype=jnp.int32)
    theta = ((j[:, None] * k[None, :]) % n).astype(F32) * (2.0 * math.pi / n)
    return jnp.cos(theta), jnp.sin(theta)


def _folded_seq_dft_matrix(s):
    half, fine = s // 2, 64
    sp = jnp.arange(s, dtype=jnp.int32)
    ka = fine * jnp.arange(half // fine, dtype=jnp.int32)
    kb = jnp.arange(fine, dtype=jnp.int32)
    ta = ((sp[:, None] * ka[None, :]) % s).astype(F32) * (2.0 * math.pi / s)
    tb = ((sp[:, None] * kb[None, :]) % s).astype(F32) * (2.0 * math.pi / s)
    ca, sa = jnp.cos(ta)[:, :, None], jnp.sin(ta)[:, :, None]
    cb, sb = jnp.cos(tb)[:, None, :], jnp.sin(tb)[:, None, :]
    cos = (ca * cb - sa * sb).reshape(s, half)
    sin = (sa * cb + ca * sb).reshape(s, half)
    sign = jnp.where(sp % 2 == 0, 1.0, -1.0).astype(F32)[:, None]
    neg_sin = jnp.where(jnp.arange(half)[None, :] == 0, sign, -sin)
    return jnp.concatenate([cos, neg_sin], axis=1).astype(BF16)


def _fourier_mixer(x, norm_gain, w_out16):
    b, s, d = x.shape
    cc, sc = _dft_tables(FN_GROUP_DIM, FN_GROUP_DIM, FN_GROUP_DIM)
    cs_c = jnp.concatenate([cc, sc], axis=1).astype(BF16)
    pq = _chan_dft(x, norm_gain, cs_c).reshape(b, s, d)
    mixed = _seq_dft(_folded_seq_dft_matrix(s), pq, scale=float((s * FN_GROUP_DIM) ** -0.5))
    return _mm_res(mixed.reshape(b * s, d), w_out16, x.reshape(b * s, d)).reshape(b, s, d)


def kernel(x_prompt, x_sample, ffn1_norm, ffn1_w_gate, ffn1_w_up, ffn1_w_down, mix_norm, dn_w_in, dn_conv_w, dn_a_log, dn_dt_bias, dn_out_norm, dn_w_out, fn_w_out, ffn2_norm, ffn2_w_gate, ffn2_w_up, ffn2_w_down, final_norm):
    depth = ffn1_norm.shape[0]
    n_mixers = 2
    f1 = [(ffn1_w_gate[i].astype(BF16), ffn1_w_up[i].astype(BF16), ffn1_w_down[i].astype(BF16)) for i in range(depth)]
    f2 = [(ffn2_w_gate[i].astype(BF16), ffn2_w_up[i].astype(BF16), ffn2_w_down[i].astype(BF16)) for i in range(depth)]
    dn_w_in16 = dn_w_in.astype(BF16)
    dn_w_out16 = dn_w_out.astype(BF16)
    fn_w_out16 = fn_w_out.astype(BF16)

    def ffn(x, gain, ws, final):
        b, s, d = x.shape
        return _ffn(x.reshape(b * s, d), gain, *ws, final_norm, final_norm=final).reshape(b, s, d)

    def trunk(x):
        for i in range(depth):
            x = ffn(x, ffn1_norm[i], f1[i], False)
            j = i // n_mixers
            if i % n_mixers == 0:
                x = _gated_deltanet(x, mix_norm[i], dn_w_in16[j], dn_conv_w[j], dn_a_log[j], dn_dt_bias[j],
                                    dn_out_norm[j], dn_w_out16[j])
            else:
                x = _fourier_mixer(x, mix_norm[i], fn_w_out16[j])
            x = ffn(x, ffn2_norm[i], f2[i], i == depth - 1)
        return x

    return (trunk(x_prompt), trunk(x_sample))
```

```python
import functools
import math

import jax
import jax.numpy as jnp
from jax import lax
from jax.experimental import pallas as pl
from jax.experimental.pallas import tpu as pltpu

EPS = 1e-6
CHUNK = 64
HEAD_DIM = 128
FN_GROUP_DIM = 256
CONV_WIDTH = 5
CONV_PAD = (CONV_WIDTH - 1) // 2
HALO = 8
VMEM_LIMIT_BYTES = 60 * 1024 * 1024
BF16 = jnp.bfloat16
F32 = jnp.float32
HIGHEST = lax.Precision.HIGHEST


def _params(*semantics):
    return pltpu.CompilerParams(dimension_semantics=semantics, vmem_limit_bytes=VMEM_LIMIT_BYTES)


def _rms_scale(x):
    return lax.rsqrt(jnp.mean(x * x, axis=-1, keepdims=True) + EPS)


def _dot(a, b):
    return jnp.dot(a, b, preferred_element_type=F32)


def _dot_nt(a, b):
    return lax.dot_general(a, b, (((1,), (1,)), ((), ())), preferred_element_type=F32)


def _ffn_body(x_ref, g_ref, wg_ref, wu_ref, wd_ref, fg_ref, o_ref, h_ref, *, n_f, final_norm):
    f = pl.program_id(1)

    @pl.when(f == 0)
    def _():
        x = x_ref[...]
        h_ref[...] = (x * _rms_scale(x) * g_ref[...]).astype(BF16)
        o_ref[...] = x

    h = h_ref[...]
    a = _dot(h, wg_ref[...])
    b = _dot(h, wu_ref[...])
    act = (a * jax.nn.sigmoid(a) * (b * 0.5)).astype(BF16)
    o_ref[...] += _dot(act, wd_ref[...])

    if final_norm:
        @pl.when(f == n_f - 1)
        def _():
            y = o_ref[...]
            o_ref[...] = y * _rms_scale(y) * fg_ref[...]


def _ffn(x, gain, wg, wu, wd, final_gain, *, final_norm, tm=1024, tf=512):
    m, d = x.shape
    f_dim = wg.shape[1]
    tm = min(tm, m)
    assert m % tm == 0 and f_dim % tf == 0
    n_f = f_dim // tf
    return pl.pallas_call(
        functools.partial(_ffn_body, n_f=n_f, final_norm=final_norm),
        grid=(m // tm, n_f),
        in_specs=[
            pl.BlockSpec((tm, d), lambda i, f: (i, 0)),
            pl.BlockSpec((1, d), lambda i, f: (0, 0)),
            pl.BlockSpec((d, tf), lambda i, f: (0, f)),
            pl.BlockSpec((d, tf), lambda i, f: (0, f)),
            pl.BlockSpec((tf, d), lambda i, f: (f, 0)),
            pl.BlockSpec((1, d), lambda i, f: (0, 0)),
        ],
        out_specs=pl.BlockSpec((tm, d), lambda i, f: (i, 0)),
        out_shape=jax.ShapeDtypeStruct((m, d), F32),
        scratch_shapes=[pltpu.VMEM((tm, d), BF16)],
        compiler_params=_params("parallel", "arbitrary"),
        name="ffn",
    )(x, gain.reshape(1, d), wg, wu, wd, final_gain.reshape(1, d))


def _norm_mm_body(x_ref, g_ref, w_ref, wn_ref, o_ref, on_ref, h_ref):
    @pl.when(pl.program_id(1) == 0)
    def _():
        x = x_ref[...]
        h = (x * _rms_scale(x) * g_ref[...]).astype(BF16)
        h_ref[...] = h
        on_ref[...] = _dot(h, wn_ref[...])

    o_ref[...] = _dot(h_ref[...], w_ref[...]).astype(o_ref.dtype)


def _norm_mm(x, gain, w, w_narrow, *, tm=1024, tn=1024):
    m, d = x.shape
    n, nn = w.shape[1], w_narrow.shape[1]
    tm, tn = min(tm, m), min(tn, n)
    assert m % tm == 0 and n % tn == 0
    return pl.pallas_call(
        _norm_mm_body,
        grid=(m // tm, n // tn),
        in_specs=[
            pl.BlockSpec((tm, d), lambda i, j: (i, 0)),
            pl.BlockSpec((1, d), lambda i, j: (0, 0)),
            pl.BlockSpec((d, tn), lambda i, j: (0, j)),
            pl.BlockSpec((d, nn), lambda i, j: (0, 0)),
        ],
        out_specs=[pl.BlockSpec((tm, tn), lambda i, j: (i, j)), pl.BlockSpec((tm, nn), lambda i, j: (i, 0))],
        out_shape=[jax.ShapeDtypeStruct((m, n), BF16), jax.ShapeDtypeStruct((m, nn), F32)],
        scratch_shapes=[pltpu.VMEM((tm, d), BF16)],
        compiler_params=_params("parallel", "arbitrary"),
        name="norm_mm",
    )(x, gain.reshape(1, d), w, w_narrow)


def _mm_res_body(a_ref, w_ref, r_ref, o_ref):
    k = pl.program_id(2)
    p = _dot(a_ref[...], w_ref[...])

    @pl.when(k == 0)
    def _():
        o_ref[...] = r_ref[...] + p

    @pl.when(k > 0)
    def _():
        o_ref[...] += p


def _mm_res(a, w, res, *, tm=1024, tn=2048, tk=1024):
    m, kd = a.shape
    n = w.shape[1]
    tm, tn, tk = min(tm, m), min(tn, n), min(tk, kd)
    assert m % tm == 0 and n % tn == 0 and kd % tk == 0
    return pl.pallas_call(
        _mm_res_body,
        grid=(m // tm, n // tn, kd // tk),
        in_specs=[
            pl.BlockSpec((tm, tk), lambda i, j, k: (i, k)),
            pl.BlockSpec((tk, tn), lambda i, j, k: (k, j)),
            pl.BlockSpec((tm, tn), lambda i, j, k: (i, j)),
        ],
        out_specs=pl.BlockSpec((tm, tn), lambda i, j, k: (i, j)),
        out_shape=jax.ShapeDtypeStruct((m, n), F32),
        compiler_params=_params("parallel", "parallel", "arbitrary"),
        name="mm_res",
    )(a, w, res)


def _gates_body(x_ref, alog_ref, dtb_ref, ga_ref, *, n_chunks):
    lane = lax.broadcasted_iota(jnp.int32, (CHUNK, 128), 1)
    is_decay = (lane % 64) < 32
    is_bwd = lane >= 64
    row = lax.broadcasted_iota(jnp.int32, (CHUNK, CHUNK), 0)
    col = lax.broadcasted_iota(jnp.int32, (CHUNK, CHUNK), 1)
    lower = (row >= col).astype(F32)
    upper = (row <= col).astype(F32)
    neg_a = -jnp.exp(alog_ref[...])
    dtb = dtb_ref[...]
    for c in range(n_chunks):
        x = x_ref[0, c * CHUNK:(c + 1) * CHUNK, :]
        t = x + dtb
        softplus = jnp.maximum(t, 0.0) + jnp.log(1.0 + jnp.exp(-jnp.abs(t)))
        g = jnp.where(is_decay, neg_a * softplus, 0.0)
        beta = jax.nn.sigmoid(x)
        cum_f = jnp.dot(lower, g, precision=HIGHEST, preferred_element_type=F32)
        cum_b = jnp.dot(upper, g, precision=HIGHEST, preferred_element_type=F32)
        cum = jnp.where(is_bwd, cum_b, cum_f)
        ga_ref[0, c * CHUNK:(c + 1) * CHUNK, :] = jnp.where(is_decay, cum, beta)


def _gates(gates, a_log, dt_bias, *, tb=512):
    b, t, w = gates.shape
    tb = min(tb, t)
    assert t % tb == 0 and tb % CHUNK == 0 and w == 128
    zeros = jnp.zeros((32,), F32)
    alog_l = jnp.concatenate([a_log[0], zeros, a_log[1], zeros]).reshape(1, 128).astype(F32)
    dtb_l = jnp.concatenate([dt_bias[0], zeros, dt_bias[1], zeros]).reshape(1, 128).astype(F32)
    spec = pl.BlockSpec((1, tb, 128), lambda i, j: (i, j, 0))
    vec = pl.BlockSpec((1, 128), lambda i, j: (0, 0))
    return pl.pallas_call(
        functools.partial(_gates_body, n_chunks=tb // CHUNK),
        grid=(b, t // tb),
        in_specs=[spec, vec, vec],
        out_specs=spec,
        out_shape=jax.ShapeDtypeStruct((b, t, 128), F32),
        compiler_params=_params("parallel", "parallel"),
        name="gates",
    )(gates, alog_l, dtb_l)


def _conv_body(x_ref, w_ref, o_ref, pad_ref, *, t, rows, l2_scale):
    lanes = x_ref.shape[-1]
    zero = jnp.zeros((HALO, lanes), F32)
    pad_ref[0:HALO, :] = zero
    pad_ref[HALO + t:HALO + t + HALO, :] = zero
    pad_ref[HALO:HALO + t, :] = x_ref[0].astype(F32)
    w = w_ref[...]

    def tile(i, carry):
        r0 = pl.multiple_of(i * rows, rows)
        win = pad_ref[pl.ds(r0, rows + 2 * HALO), :]
        acc = None
        for j in range(CONV_WIDTH):
            s = HALO - CONV_PAD + j
            term = win[s:s + rows, :] * w[j:j + 1, :]
            acc = term if acc is None else acc + term
        y = acc * jax.nn.sigmoid(acc)
        if l2_scale is not None:
            parts = []
            for h in range(lanes // HEAD_DIM):
                yh = y[:, h * HEAD_DIM:(h + 1) * HEAD_DIM]
                inv = lax.rsqrt(jnp.sum(yh * yh, axis=-1, keepdims=True) + EPS)
                parts.append(yh * (inv * l2_scale))
            y = jnp.concatenate(parts, axis=-1)
        o_ref[0, pl.ds(r0, rows), :] = y.astype(o_ref.dtype)
        return carry

    lax.fori_loop(0, t // rows, tile, 0)


def _conv(proj, conv_w, *, col0, width, l2_scale, lanes=256, rows=256):
    b, t, _ = proj.shape
    rows = min(rows, t)
    assert width % lanes == 0 and col0 % lanes == 0 and t % rows == 0
    off = col0 // lanes
    w8 = jnp.zeros((8, width), F32).at[:CONV_WIDTH].set(conv_w.astype(F32))
    return pl.pallas_call(
        functools.partial(_conv_body, t=t, rows=rows, l2_scale=l2_scale),
        grid=(b, width // lanes),
        in_specs=[
            pl.BlockSpec((1, t, lanes), lambda i, j: (i, 0, j + off)),
            pl.BlockSpec((8, lanes), lambda i, j: (0, j)),
        ],
        out_specs=pl.BlockSpec((1, t, lanes), lambda i, j: (i, 0, j)),
        out_shape=jax.ShapeDtypeStruct((b, t, width), BF16),
        scratch_shapes=[pltpu.VMEM((t + 2 * HALO, lanes), F32)],
        compiler_params=_params("parallel", "parallel"),
        name="conv",
    )(proj, w8)


N_SUB = 4
SUPER = N_SUB * CHUNK
KH_STEP = 2


def _pack(mat):
    blk = lax.broadcasted_iota(jnp.int32, (CHUNK, SUPER), 1) // CHUNK
    out = mat[(N_SUB - 1) * CHUNK:, :]
    for n in range(N_SUB - 2, -1, -1):
        out = jnp.where(blk == n, mat[n * CHUNK:(n + 1) * CHUNK, :], out)
    return out


def _pack_col(colv):
    blk = lax.broadcasted_iota(jnp.int32, (CHUNK, SUPER), 1) // CHUNK
    out = jnp.broadcast_to(colv[(N_SUB - 1) * CHUNK:, :], (CHUNK, SUPER))
    for n in range(N_SUB - 2, -1, -1):
        out = jnp.where(blk == n, jnp.broadcast_to(colv[n * CHUNK:(n + 1) * CHUNK, :], (CHUNK, SUPER)), out)
    return out


def _block_diag(packed):
    r = lax.broadcasted_iota(jnp.int32, (SUPER, SUPER), 0) // CHUNK
    c = lax.broadcasted_iota(jnp.int32, (SUPER, SUPER), 1) // CHUNK
    return jnp.where(r == c, jnp.concatenate([packed] * N_SUB, axis=0), jnp.zeros((), packed.dtype))


BASE = 8


def _unit_tri_inverse_packed(lps):
    i = lax.broadcasted_iota(jnp.int32, (CHUNK, SUPER), 0)
    j = lax.broadcasted_iota(jnp.int32, (CHUNK, SUPER), 1) % CHUNK
    eye = jnp.where(i == j, 1.0, 0.0)

    def bd16(m):
        return _block_diag(m.astype(BF16))

    inside = (i // BASE) == (j // BASE)
    xs = [jnp.where(inside, -lp, 0.0) for lp in lps]
    ts = [eye + x for x in xs]
    n_levels = int(math.log2(BASE)) - 1
    xs = [_dot(x.astype(BF16), bd16(x)) for x in xs]
    for lvl in range(n_levels):
        rhss = [bd16(x) for x in xs]
        if lvl == n_levels - 1:
            ts = [t + _dot(t.astype(BF16), rhs) for t, rhs in zip(ts, rhss)]
        else:
            boths = [_dot(jnp.concatenate([x, t], axis=0).astype(BF16), rhs) for x, t, rhs in zip(xs, ts, rhss)]
            xs = [both[:CHUNK] for both in boths]
            ts = [t + both[CHUNK:] for t, both in zip(ts, boths)]
    width = BASE
    while width < CHUNK:
        joins = jnp.logical_and((i // (2 * width)) == (j // (2 * width)), (i // width) != (j // width))
        ys = [_dot(jnp.where(joins, lp, 0.0).astype(BF16), bd16(t)) for lp, t in zip(lps, ts)]
        zs = [_dot(t.astype(BF16), bd16(y)) for t, y in zip(ts, ys)]
        ts = [t - z for t, z in zip(ts, zs)]
        width *= 2
    return ts


def _delta_body(q_ref, k_ref, v_ref, z_ref, ga_ref, gr_ref, gain_ref, y_ref, o_ref, s_ref,
                *, n_super, rows):
    kh0 = pl.program_id(1) * KH_STEP
    t = n_super * SUPER
    o_ref[...] = jnp.zeros_like(o_ref)
    s_ref[...] = jnp.zeros_like(s_ref)

    pi = lax.broadcasted_iota(jnp.int32, (CHUNK, SUPER), 0)
    pj = lax.broadcasted_iota(jnp.int32, (CHUNK, SUPER), 1) % CHUNK
    gate_lane = lax.broadcasted_iota(jnp.int32, (SUPER, 128), 1)
    row_blk = lax.broadcasted_iota(jnp.int32, (SUPER, 1), 0) // CHUNK
    col_blk = lax.broadcasted_iota(jnp.int32, (HEAD_DIM, SUPER), 1) // CHUNK

    def super_step(s, carry):
        chains = []
        for d in range(2):
            ss = s if d == 0 else n_super - 1 - s
            r0 = pl.multiple_of(ss * SUPER, SUPER)
            incl = (pi >= pj) if d == 0 else (pi <= pj)
            strict = (pi > pj) if d == 0 else (pi < pj)
            ga = ga_ref[0, pl.ds(r0, SUPER), :]
            for kk in range(KH_STEP):
                q16 = q_ref[0, pl.ds(r0, SUPER), kk * HEAD_DIM:(kk + 1) * HEAD_DIM]
                k16 = k_ref[0, pl.ds(r0, SUPER), kk * HEAD_DIM:(kk + 1) * HEAD_DIM]
                grows = gr_ref[0, kk, ss]
                gram = _dot_nt(jnp.concatenate([k16, q16], axis=0), k16)
                kk_p = _pack(gram[:SUPER])
                qk_p = _pack(gram[SUPER:])
                q = q16.astype(F32)
                k = k16.astype(F32)
                for j in range(2):
                    hv = 2 * kk + j
                    lane_g = d * 64 + 2 * kh0 + hv
                    gcol = jnp.sum(jnp.where(gate_lane == lane_g, ga, 0.0), axis=-1, keepdims=True)
                    bcol = jnp.sum(jnp.where(gate_lane == lane_g + 32, ga, 0.0), axis=-1, keepdims=True)
                    grow = grows[d * 2 + j:d * 2 + j + 1, :]
                    diff = _pack_col(gcol) - grow
                    decay = jnp.where(incl, jnp.exp(jnp.where(incl, diff, 0.0)), 0.0)
                    eg = jnp.exp(gcol)
                    v = v_ref[0, pl.ds(r0, SUPER), hv * HEAD_DIM:(hv + 1) * HEAD_DIM].astype(F32)
                    ends = [n * CHUNK + (CHUNK - 1 if d == 0 else 0) for n in range(N_SUB)]
                    tot = [gcol[e:e + 1, :] for e in ends]
                    totcol = jnp.broadcast_to(tot[N_SUB - 1], (SUPER, 1))
                    for n in range(N_SUB - 2, -1, -1):
                        totcol = jnp.where(row_blk == n, tot[n], totcol)
                    kd_t = jnp.transpose(k * jnp.exp(totcol - gcol))
                    qkm_bd = _block_diag(jnp.where(incl, qk_p * decay, 0.0))
                    pre_lhs = jnp.concatenate(
                        [jnp.where(col_blk == n, kd_t, 0.0) for n in range(N_SUB)] + [qkm_bd], axis=0).astype(BF16)
                    chains.append(dict(
                        d=d, hv=hv, r0=r0, tot=tot, pre_lhs=pre_lhs,
                        lmat=jnp.where(strict, _pack_col(bcol) * kk_p * decay, 0.0),
                        rhs=jnp.concatenate([v * bcol, k * (bcol * eg)], axis=1).astype(BF16),
                        qe=q * eg,
                        state=s_ref[d * 2 * KH_STEP + hv],
                        outs=[None] * N_SUB))
        tmats = _unit_tri_inverse_packed([c["lmat"] for c in chains])
        uws = [_dot(_block_diag(tmat.astype(BF16)), c["rhs"]).astype(BF16) for c, tmat in zip(chains, tmats)]
        pres = [_dot(c["pre_lhs"], uw) for c, uw in zip(chains, uws)]
        for c, pre in zip(chains, pres):
            c["b"] = [pre[n * HEAD_DIM:(n + 1) * HEAD_DIM, :HEAD_DIM] for n in range(N_SUB)]
            qw = pre[N_SUB * HEAD_DIM:, HEAD_DIM:]
            c["o0"] = pre[N_SUB * HEAD_DIM:, :HEAD_DIM]
            c["lhs"] = [jnp.concatenate([-pre[n * HEAD_DIM:(n + 1) * HEAD_DIM, HEAD_DIM:],
                                         c["qe"][n * CHUNK:(n + 1) * CHUNK] - qw[n * CHUNK:(n + 1) * CHUNK]],
                                        axis=0).astype(BF16) for n in range(N_SUB)]
        for step in range(N_SUB):
            order = [step if c["d"] == 0 else N_SUB - 1 - step for c in chains]
            prods = [_dot(c["lhs"][n], c["state"].astype(BF16)) for c, n in zip(chains, order)]
            for c, n, prod in zip(chains, order, prods):
                c["outs"][n] = prod[HEAD_DIM:] + c["o0"][n * CHUNK:(n + 1) * CHUNK]
                c["state"] = c["state"] * jnp.exp(c["tot"][n]) + (prod[:HEAD_DIM] + c["b"][n])
        for c in chains:
            lanes = slice(c["hv"] * HEAD_DIM, (c["hv"] + 1) * HEAD_DIM)
            o_ref[pl.ds(c["r0"], SUPER), lanes] += jnp.concatenate(c["outs"], axis=0)
            s_ref[c["d"] * 2 * KH_STEP + c["hv"]] = c["state"]
        return carry

    lax.fori_loop(0, n_super, super_step, 0)

    gain = gain_ref[...]

    def norm_tile(i, carry):
        r0 = pl.multiple_of(i * rows, rows)
        o = o_ref[pl.ds(r0, rows), :]
        z = z_ref[0, pl.ds(r0, rows), :].astype(F32)
        parts = []
        for hv in range(2 * KH_STEP):
            oh = o[:, hv * HEAD_DIM:(hv + 1) * HEAD_DIM]
            parts.append(oh * _rms_scale(oh) * gain)
        y = jnp.concatenate(parts, axis=-1) * (z * jax.nn.sigmoid(z))
        y_ref[0, pl.ds(r0, rows), :] = y.astype(y_ref.dtype)
        return carry

    lax.fori_loop(0, t // rows, norm_tile, 0)


def _gate_rows(ga, n_kh):
    b, t, _ = ga.shape
    fwd = ga[..., 0:2 * n_kh].reshape(b, t, n_kh, 1, 2)
    bwd = ga[..., 64:64 + 2 * n_kh].reshape(b, t, n_kh, 1, 2)
    g4 = jnp.concatenate([fwd, bwd], axis=3).reshape(b, t // SUPER, SUPER, n_kh, 4)
    g4 = jnp.transpose(g4, (0, 3, 1, 4, 2))
    return jnp.concatenate([g4, jnp.zeros_like(g4)], axis=3)


def _delta(q, k, v, proj, z_col0, ga, out_gain, *, rows=256):
    b, t, kdim = q.shape
    n_kh = kdim // HEAD_DIM
    vdim = v.shape[-1]
    kw, vw = KH_STEP * HEAD_DIM, 2 * KH_STEP * HEAD_DIM
    assert vdim == 2 * kdim and t % SUPER == 0 and z_col0 % vw == 0 and 2 * n_kh == 32 and n_kh % KH_STEP == 0
    rows = min(rows, t)
    n_super = t // SUPER
    z_off = z_col0 // vw
    gr = _gate_rows(ga, n_kh)
    return pl.pallas_call(
        functools.partial(_delta_body, n_super=n_super, rows=rows),
        grid=(b, n_kh // KH_STEP),
        in_specs=[
            pl.BlockSpec((1, t, kw), lambda i, h: (i, 0, h)),
            pl.BlockSpec((1, t, kw), lambda i, h: (i, 0, h)),
            pl.BlockSpec((1, t, vw), lambda i, h: (i, 0, h)),
            pl.BlockSpec((1, t, vw), lambda i, h: (i, 0, h + z_off)),
            pl.BlockSpec((1, t, 128), lambda i, h: (i, 0, 0)),
            pl.BlockSpec((1, KH_STEP, n_super, 8, SUPER), lambda i, h: (i, h, 0, 0, 0)),
            pl.BlockSpec((1, HEAD_DIM), lambda i, h: (0, 0)),
        ],
        out_specs=pl.BlockSpec((1, t, vw), lambda i, h: (i, 0, h)),
        out_shape=jax.ShapeDtypeStruct((b, t, vdim), BF16),
        scratch_shapes=[pltpu.VMEM((t, vw), F32), pltpu.VMEM((4 * KH_STEP, HEAD_DIM, HEAD_DIM), F32)],
        compiler_params=_params("parallel", "arbitrary"),
        name="delta",
    )(q, k, v, proj, ga, gr, out_gain.reshape(1, HEAD_DIM).astype(F32))


def _gated_deltanet(x, norm_gain, w_in16, conv_w, a_log, dt_bias, out_norm, w_out16):
    b, s, d = x.shape
    n_gate = 128
    qkvz_dim = w_in16.shape[1] - n_gate
    value_dim = w_out16.shape[0]
    qkv_dim = qkvz_dim - value_dim
    key_dim = (qkv_dim - value_dim) // 2
    x2 = x.reshape(b * s, d)
    proj, gates = _norm_mm(x2, norm_gain, w_in16[:, :qkvz_dim], w_in16[:, qkvz_dim:])
    proj, gates = proj.reshape(b, s, qkvz_dim), gates.reshape(b, s, n_gate)
    ga = _gates(gates, a_log, dt_bias)
    q = _conv(proj, conv_w[:, :key_dim], col0=0, width=key_dim, l2_scale=HEAD_DIM ** -0.5)
    k = _conv(proj, conv_w[:, key_dim:2 * key_dim], col0=key_dim, width=key_dim, l2_scale=1.0)
    v = _conv(proj, conv_w[:, 2 * key_dim:], col0=2 * key_dim, width=value_dim, l2_scale=None)
    y = _delta(q, k, v, proj, qkv_dim, ga, out_norm)
    return _mm_res(y.reshape(b * s, value_dim), w_out16, x2).reshape(b, s, d)


def _chan_dft_body(x_ref, xp_ref, xq_ref, xm_ref, g_ref, rev_ref, cs_ref, o_ref):
    first = pl.program_id(1) == 0
    g = g_ref[...]
    x = x_ref[0]
    xp = xp_ref[0]
    xq = xq_ref[0]
    h = x * _rms_scale(x) * g
    hp = (xp * _rms_scale(xp) * g).astype(BF16)
    hq = xq * _rms_scale(xq) * g
    row = lax.broadcasted_iota(jnp.int32, h.shape, 0)
    hr = jnp.where(row == 0, hq[0:1, :], _dot(rev_ref[...], hp))
    he = jnp.where(jnp.logical_and(first, row == 0), h, h + hr).astype(BF16)
    ho = (h - hr).astype(BF16)
    cs = cs_ref[...]
    for gi in range(x.shape[-1] // FN_GROUP_DIM):
        lo, hi = gi * FN_GROUP_DIM, (gi + 1) * FN_GROUP_DIM
        o_ref[0, 0, :, lo:hi] = _dot(he[:, lo:hi], cs[:, :FN_GROUP_DIM]).astype(o_ref.dtype)
        o_ref[0, 1, :, lo:hi] = _dot(ho[:, lo:hi], cs[:, FN_GROUP_DIM:]).astype(o_ref.dtype)

    @pl.when(first)
    def _():
        xm = xm_ref[0]
        hm = (xm * _rms_scale(xm) * g).astype(BF16)
        row8 = lax.broadcasted_iota(jnp.int32, (8, FN_GROUP_DIM), 0)
        for gi in range(x.shape[-1] // FN_GROUP_DIM):
            lo, hi = gi * FN_GROUP_DIM, (gi + 1) * FN_GROUP_DIM
            pm = _dot(hm[:, lo:hi], cs[:, :FN_GROUP_DIM]).astype(o_ref.dtype)
            o_ref[0, 1, 0:8, lo:hi] = jnp.where(row8 == 0, pm, o_ref[0, 1, 0:8, lo:hi])


def _chan_dft(x, gain, cs, *, tm=512):
    b, s, d = x.shape
    half = s // 2
    tm = min(tm, half)
    assert half % tm == 0 and d % FN_GROUP_DIM == 0 and tm % 8 == 0
    n_blk = s // tm
    ri = jnp.arange(tm, dtype=jnp.int32)
    rev = (ri[:, None] + ri[None, :] == tm).astype(BF16)
    row8 = lambda i, j: (i, ((s - j * tm) % s) // 8, 0)
    return pl.pallas_call(
        _chan_dft_body,
        grid=(b, half // tm),
        in_specs=[
            pl.BlockSpec((1, tm, d), lambda i, j: (i, j, 0)),
            pl.BlockSpec((1, tm, d), lambda i, j: (i, n_blk - 1 - j, 0)),
            pl.BlockSpec((1, 8, d), row8),
            pl.BlockSpec((1, 8, d), lambda i, j: (i, half // 8, 0)),
            pl.BlockSpec((1, d), lambda i, j: (0, 0)),
            pl.BlockSpec((tm, tm), lambda i, j: (0, 0)),
            pl.BlockSpec((FN_GROUP_DIM, 2 * FN_GROUP_DIM), lambda i, j: (0, 0)),
        ],
        out_specs=pl.BlockSpec((1, 2, tm, d), lambda i, j: (i, 0, j, 0)),
        out_shape=jax.ShapeDtypeStruct((b, 2, half, d), BF16),
        compiler_params=_params("parallel", "arbitrary"),
        name="chan_dft",
    )(x, x, x, x, gain.reshape(1, d), rev, cs)


def _seq_dft_body(a_ref, b_ref, o_ref, acc_ref, *, n_k, scale):
    k = pl.program_id(3)
    p = _dot(a_ref[...], b_ref[0])

    @pl.when(k == 0)
    def _():
        acc_ref[...] = p

    @pl.when(k > 0)
    def _():
        acc_ref[...] += p

    @pl.when(k == n_k - 1)
    def _():
        o_ref[0] = (acc_ref[...] * scale).astype(o_ref.dtype)


def _seq_dft(dft, pq, *, scale, tm=2048, tn=1024, tk=1024):
    s, k2 = dft.shape
    b, _, d = pq.shape
    tm, tn, tk = min(tm, s), min(tn, d), min(tk, k2)
    assert s % tm == 0 and d % tn == 0 and k2 % tk == 0
    n_k = k2 // tk
    return pl.pallas_call(
        functools.partial(_seq_dft_body, n_k=n_k, scale=scale),
        grid=(b, s // tm, d // tn, n_k),
        in_specs=[
            pl.BlockSpec((tm, tk), lambda i, m, n, k: (m, k)),
            pl.BlockSpec((1, tk, tn), lambda i, m, n, k: (i, k, n)),
        ],
        out_specs=pl.BlockSpec((1, tm, tn), lambda i, m, n, k: (i, m, n)),
        out_shape=jax.ShapeDtypeStruct((b, s, d), BF16),
        scratch_shapes=[pltpu.VMEM((tm, tn), F32)],
        compiler_params=_params("parallel", "parallel", "parallel", "arbitrary"),
        name="seq_dft",
    )(dft, pq)


def _dft_tables(n_rows, n_cols, n):
    j = jnp.arange(n_rows, dtype=jnp.int32)
    k = jnp.arange(n_cols, dtype=jnp.int32)
    theta = ((j[:, None] * k[None, :]) % n).astype(F32) * (2.0 * math.pi / n)
    return jnp.cos(theta), jnp.sin(theta)


def _folded_seq_dft_matrix(s):
    half, fine = s // 2, 64
    sp = jnp.arange(s, dtype=jnp.int32)
    ka = fine * jnp.arange(half // fine, dtype=jnp.int32)
    kb = jnp.arange(fine, dtype=jnp.int32)
    ta = ((sp[:, None] * ka[None, :]) % s).astype(F32) * (2.0 * math.pi / s)
    tb = ((sp[:, None] * kb[None, :]) % s).astype(F32) * (2.0 * math.pi / s)
    ca, sa = jnp.cos(ta)[:, :, None], jnp.sin(ta)[:, :, None]
    cb, sb = jnp.cos(tb)[:, None, :], jnp.sin(tb)[:, None, :]
    cos = (ca * cb - sa * sb).reshape(s, half)
    sin = (sa * cb + ca * sb).reshape(s, half)
    sign = jnp.where(sp % 2 == 0, 1.0, -1.0).astype(F32)[:, None]
    neg_sin = jnp.where(jnp.arange(half)[None, :] == 0, sign, -sin)
    return jnp.concatenate([cos, neg_sin], axis=1).astype(BF16)


def _fourier_mixer(x, norm_gain, w_out16):
    b, s, d = x.shape
    cc, sc = _dft_tables(FN_GROUP_DIM, FN_GROUP_DIM, FN_GROUP_DIM)
    cs_c = jnp.concatenate([cc, sc], axis=1).astype(BF16)
    pq = _chan_dft(x, norm_gain, cs_c).reshape(b, s, d)
    mixed = _seq_dft(_folded_seq_dft_matrix(s), pq, scale=float((s * FN_GROUP_DIM) ** -0.5))
    return _mm_res(mixed.reshape(b * s, d), w_out16, x.reshape(b * s, d)).reshape(b, s, d)


def kernel(x_prompt, x_sample, ffn1_norm, ffn1_w_gate, ffn1_w_up, ffn1_w_down, mix_norm, dn_w_in, dn_conv_w, dn_a_log, dn_dt_bias, dn_out_norm, dn_w_out, fn_w_out, ffn2_norm, ffn2_w_gate, ffn2_w_up, ffn2_w_down, final_norm):
    depth = ffn1_norm.shape[0]
    n_mixers = 2
    f1 = [(ffn1_w_gate[i].astype(BF16), ffn1_w_up[i].astype(BF16), ffn1_w_down[i].astype(BF16)) for i in range(depth)]
    f2 = [(ffn2_w_gate[i].astype(BF16), ffn2_w_up[i].astype(BF16), ffn2_w_down[i].astype(BF16)) for i in range(depth)]
    dn_w_in16 = dn_w_in.astype(BF16)
    dn_w_out16 = dn_w_out.astype(BF16)
    fn_w_out16 = fn_w_out.astype(BF16)

    def ffn(x, gain, ws, final):
        b, s, d = x.shape
        return _ffn(x.reshape(b * s, d), gain, *ws, final_norm, final_norm=final).reshape(b, s, d)

    def trunk(x):
        for i in range(depth):
            x = ffn(x, ffn1_norm[i], f1[i], False)
            j = i // n_mixers
            if i % n_mixers == 0:
                x = _gated_deltanet(x, mix_norm[i], dn_w_in16[j], dn_conv_w[j], dn_a_log[j], dn_dt_bias[j],
                                    dn_out_norm[j], dn_w_out16[j])
            else:
                x = _fourier_mixer(x, mix_norm[i], fn_w_out16[j])
            x = ffn(x, ffn2_norm[i], f2[i], i == depth - 1)
        return x

    return (trunk(x_prompt), trunk(x_sample))
```
